```python
import math
import jax
import jax.numpy as jnp
from jax import lax
import numpy as np

D_MODEL = 1024
BATCH = 4
SEQ = 4096
DEPTH = 2
DEC_BATCH = 32
DEC_SEQ = 8
PAST_LEN = 8192
PAGE_SIZE = 128

N_A_LAYERS = DEPTH // 2
N_B_LAYERS = DEPTH - N_A_LAYERS
EPS = 1e-6
CONV_WIDTH = 31
N_HEADS = 16
HEAD_DIM = D_MODEL // N_HEADS
N_KV_HEADS = 4
HEADS_PER_KV = N_HEADS // N_KV_HEADS
N_BRANCH = 3
CMP_BLOCK = 32
CMP_STRIDE = 16
CMP_HIDDEN = 2 * HEAD_DIM
SLC_BLOCK = 64
N_SELECT = 16
WINDOW = 512
Q_BLOCK = 32
PEER_HEADS = 8
PEER_KEYS = 128
PEER_EXPERTS = PEER_KEYS * PEER_KEYS
PEER_TOPK = 16
PEER_DKEY = 256
PEER_CHUNK = 128

kernel_name = 'yoco_conformer_nsa_peer_step'


def rmsnorm(x, g):
    xf = x.astype(jnp.float32)
    y = xf * lax.rsqrt(jnp.mean(xf * xf, axis=-1, keepdims=True) + EPS)
    return (y * g.astype(jnp.float32)).astype(x.dtype)


def layernorm(x, g, b):
    xf = x.astype(jnp.float32)
    mu = jnp.mean(xf, axis=-1, keepdims=True)
    var = jnp.mean(jnp.square(xf - mu), axis=-1, keepdims=True)
    y = (xf - mu) * lax.rsqrt(var + EPS)
    return (y * g.astype(jnp.float32) + b.astype(jnp.float32)).astype(x.dtype)


def masked_softmax(s, mask):
    s = jnp.where(mask, s, -jnp.inf)
    m = jnp.max(s, axis=-1, keepdims=True)
    m = jnp.where(jnp.isfinite(m), m, 0.0)
    e = jnp.where(mask, jnp.exp(s - m), 0.0)
    return e / jnp.maximum(jnp.sum(e, axis=-1, keepdims=True), 1e-30)


def conformer_conv(h, past, w_in, dw_k, dw_b, ln_g, ln_b, w_out):
    a, b = jnp.split(h @ w_in, 2, axis=-1)
    u = a * jax.nn.sigmoid(b)
    full = jnp.concatenate([past.astype(u.dtype), u], axis=1)
    y = lax.conv_general_dilated(full, dw_k.astype(u.dtype)[:, None, :], window_strides=(1,),
                                 padding='VALID', dimension_numbers=('NWC', 'WIO', 'NWC'),
                                 feature_group_count=D_MODEL)
    y = jax.nn.silu(layernorm(y + dw_b, ln_g, ln_b))
    return y @ w_out, full[:, -(CONV_WIDTH - 1):]


def peer_ffn(h, w_q, sub_k, u_emb, v_emb):
    shp = h.shape
    tok = h.reshape(-1, D_MODEL)
    n = tok.shape[0]
    tok = jnp.pad(tok, ((0, (-n) % PEER_CHUNK), (0, 0)))
    chunks = tok.reshape(-1, PEER_CHUNK, D_MODEL)
    k1 = sub_k[0].astype(jnp.float32)
    k2 = sub_k[1].astype(jnp.float32)

    def one(c):
        q = (c @ w_q).astype(jnp.float32).reshape(PEER_CHUNK, PEER_HEADS, 2, PEER_DKEY // 2)
        s1 = jnp.einsum('chd,kd->chk', q[:, :, 0], k1)
        s2 = jnp.einsum('chd,kd->chk', q[:, :, 1], k2)
        v1, i1 = lax.top_k(s1, PEER_TOPK)
        v2, i2 = lax.top_k(s2, PEER_TOPK)
        cand = (v1[..., :, None] + v2[..., None, :]).reshape(PEER_CHUNK, PEER_HEADS, PEER_TOPK * PEER_TOPK)
        cidx = (i1[..., :, None] * PEER_KEYS + i2[..., None, :]).reshape(PEER_CHUNK, PEER_HEADS, PEER_TOPK * PEER_TOPK)
        best, pos = lax.top_k(cand, PEER_TOPK)
        eidx = jnp.take_along_axis(cidx, pos, axis=-1)
        gate = jax.nn.softmax(best, axis=-1)
        act = jax.nn.gelu(jnp.einsum('chkd,cd->chk', u_emb[eidx], c).astype(jnp.float32))
        w = (gate * act).astype(c.dtype)
        return jnp.einsum('chk,chkd->cd', w, v_emb[eidx])

    out = lax.map(one, chunks).reshape(-1, D_MODEL)[:n]
    return out.reshape(shp)


def compress(x, pe, w1, w2):
    bsz, length = x.shape[:2]
    r = CMP_BLOCK // CMP_STRIDE
    n_cmp = (length - CMP_BLOCK) // CMP_STRIDE + 1
    n_chunk = n_cmp + r - 1
    chunks = x[:, :n_chunk * CMP_STRIDE].reshape(bsz, n_chunk, CMP_STRIDE, N_KV_HEADS, HEAD_DIM)
    w1r = w1.reshape(r, CMP_STRIDE, HEAD_DIM, CMP_HIDDEN)
    per = pe.reshape(r, CMP_STRIDE, HEAD_DIM)
    hid = jnp.einsum('sd,sdh->h', per[0], w1r[0]) + jnp.einsum('bnsgd,sdh->bngh', chunks[:, 0:n_cmp], w1r[0])
    for j in range(1, r):
        hid = hid + jnp.einsum('sd,sdh->h', per[j], w1r[j]) + jnp.einsum('bnsgd,sdh->bngh', chunks[:, j:j + n_cmp], w1r[j])
    return jax.nn.gelu(hid) @ w2


def gather_blocks(kb, sel):
    return jax.vmap(jax.vmap(lambda tab, idx: tab[idx]))(kb, sel)


def nsa_shared_kv(kv_cmp, kv_slc, kv_win, cmp_pe, cmp_w1, cmp_w2):
    k_c = compress(kv_cmp[:, :, 0], cmp_pe[0], cmp_w1[0], cmp_w2[0])
    v_c = compress(kv_cmp[:, :, 1], cmp_pe[1], cmp_w1[1], cmp_w2[1])
    bsz, length = kv_slc.shape[:2]
    n_slc = -(-length // SLC_BLOCK)
    kvs = jnp.pad(kv_slc, ((0, 0), (0, n_slc * SLC_BLOCK - length), (0, 0), (0, 0), (0, 0)))
    kvs = kvs.reshape(bsz, n_slc, SLC_BLOCK, 2, N_KV_HEADS, HEAD_DIM).transpose(3, 0, 4, 1, 2, 5)
    kw = jnp.pad(kv_win, ((0, 0), (WINDOW - 1, 0), (0, 0), (0, 0), (0, 0)))
    return (k_c, v_c, kvs[0], kvs[1], kw[:, :, 0], kw[:, :, 1])


def nsa_mixer(xn, shared, pos0, win_pos0, w_in, w_o):
    k_c, v_c, kb_k, kb_v, kw_k, kw_v = shared
    bsz, seq, _ = xn.shape
    f32 = jnp.float32
    proj = xn @ w_in
    q = proj[..., :N_HEADS * HEAD_DIM].reshape(bsz, seq, N_KV_HEADS, HEADS_PER_KV, HEAD_DIM)
    gates = jax.nn.sigmoid(proj[..., N_HEADS * HEAD_DIM:].astype(f32)).reshape(
        bsz, seq, N_KV_HEADS, HEADS_PER_KV, N_BRANCH)
    n_cmp = k_c.shape[1]
    n_slc = kb_k.shape[2]
    cmp_start = jnp.arange(n_cmp) * CMP_STRIDE
    cmp_end = cmp_start + CMP_BLOCK - 1
    slc_start = jnp.arange(n_slc) * SLC_BLOCK
    overlap = ((cmp_start[:, None] < slc_start[None, :] + SLC_BLOCK)
               & (cmp_end[:, None] >= slc_start[None, :])).astype(f32)
    n_sel = min(N_SELECT, n_slc)
    qb = math.gcd(Q_BLOCK, seq)
    n_blk = seq // qb
    q_blocks = jnp.moveaxis(q.reshape(bsz, n_blk, qb, N_KV_HEADS, HEADS_PER_KV, HEAD_DIM), 1, 0)
    g_blocks = jnp.moveaxis(gates.reshape(bsz, n_blk, qb, N_KV_HEADS, HEADS_PER_KV, N_BRANCH), 1, 0)
    k_c32 = k_c.astype(f32)
    v_c32 = v_c.astype(f32)
    scale = HEAD_DIM ** -0.5
    n_wkeys = WINDOW - 1 + qb

    def block(args):
        i, qblk, gblk = args
        p0 = pos0 + i * qb
        t = p0 + jnp.arange(qb)
        qf = qblk.astype(f32) * scale
        s_c = jnp.einsum('bqghd,bngd->bghqn', qf, k_c32)
        p_c = masked_softmax(s_c, cmp_end[None, :] <= t[:, None])
        o_c = jnp.einsum('bghqn,bngd->bqghd', p_c, v_c32)
        imp = jnp.einsum('bghqn,ns->bgqs', p_c, overlap)
        cur = (t // SLC_BLOCK)[:, None]
        blk = jnp.arange(n_slc)[None, :]
        forced = (blk == 0) | (blk == cur) | (blk == cur - 1)
        score = jnp.where(forced, jnp.inf, jnp.where(blk <= cur, imp, -jnp.inf))
        _, sel = lax.top_k(score, n_sel)
        k_sel = gather_blocks(kb_k, sel).astype(f32)
        v_sel = gather_blocks(kb_v, sel).astype(f32)
        kpos = sel[..., None] * SLC_BLOCK + jnp.arange(SLC_BLOCK)
        m_s = (kpos <= t[:, None, None]).reshape(bsz, N_KV_HEADS, 1, qb, n_sel * SLC_BLOCK)
        s_s = jnp.einsum('bqghd,bgqskd->bghqsk', qf, k_sel).reshape(
            bsz, N_KV_HEADS, HEADS_PER_KV, qb, n_sel * SLC_BLOCK)
        p_s = masked_softmax(s_s, m_s).reshape(bsz, N_KV_HEADS, HEADS_PER_KV, qb, n_sel, SLC_BLOCK)
        o_s = jnp.einsum('bghqsk,bgqskd->bqghd', p_s, v_sel)
        start = p0 - win_pos0
        k_w = lax.dynamic_slice_in_dim(kw_k, start, n_wkeys, axis=1).astype(f32)
        v_w = lax.dynamic_slice_in_dim(kw_v, start, n_wkeys, axis=1).astype(f32)
        wpos = p0 - (WINDOW - 1) + jnp.arange(n_wkeys)
        m_w = ((wpos[None, :] <= t[:, None]) & (wpos[None, :] > t[:, None] - WINDOW)
               & (wpos[None, :] >= win_pos0))
        p_w = masked_softmax(jnp.einsum('bqghd,bkgd->bghqk', qf, k_w), m_w)
        o_w = jnp.einsum('bghqk,bkgd->bqghd', p_w, v_w)
        o = gblk[..., 0:1] * o_c + gblk[..., 1:2] * o_s + gblk[..., 2:3] * o_w
        return o.astype(qblk.dtype)

    o = lax.map(block, (jnp.arange(n_blk), q_blocks, g_blocks))
    o = jnp.moveaxis(o, 0, 1).reshape(bsz, seq, N_HEADS * HEAD_DIM).astype(xn.dtype)
    return o @ w_o


def trunk(x, conv_past, cmp_past, slc_past, win_past, pos0,
          norm_mix_g, norm_ffn_g, norm_kv_g, norm_out_g,
          conv_w_in, conv_dw_k, conv_dw_b, conv_ln_g, conv_ln_b, conv_w_out,
          nsa_w_kv, nsa_cmp_pe, nsa_cmp_w1, nsa_cmp_w2, nsa_w_in, nsa_w_o,
          peer_w_q, peer_sub_k, peer_u, peer_v):
    h = x
    conv_states = []
    for l in range(N_A_LAYERS):
        y, st = conformer_conv(rmsnorm(h, norm_mix_g[l]), conv_past[l], conv_w_in[l], conv_dw_k[l],
                               conv_dw_b[l], conv_ln_g[l], conv_ln_b[l], conv_w_out[l])
        h = h + y
        conv_states.append(st)
        h = h + peer_ffn(rmsnorm(h, norm_ffn_g[l]), peer_w_q[l], peer_sub_k[l], peer_u[l], peer_v[l])
    bsz, seq, _ = h.shape
    kv = (rmsnorm(h, norm_kv_g) @ nsa_w_kv).reshape(bsz, seq, N_BRANCH, 2, N_KV_HEADS, HEAD_DIM)
    new_cmp, new_slc, new_win = kv[:, :, 0], kv[:, :, 1], kv[:, :, 2]
    win_all = jnp.concatenate([win_past.astype(kv.dtype), new_win], axis=1)
    shared = nsa_shared_kv(jnp.concatenate([cmp_past.astype(kv.dtype), new_cmp], axis=1),
                           jnp.concatenate([slc_past.astype(kv.dtype), new_slc], axis=1),
                           win_all, nsa_cmp_pe, nsa_cmp_w1, nsa_cmp_w2)
    win_pos0 = pos0 - win_past.shape[1]
    for l in range(N_B_LAYERS):
        li = N_A_LAYERS + l
        h = h + nsa_mixer(rmsnorm(h, norm_mix_g[li]), shared, pos0, win_pos0, nsa_w_in[l], nsa_w_o[l])
        h = h + peer_ffn(rmsnorm(h, norm_ffn_g[li]), peer_w_q[li], peer_sub_k[li], peer_u[li], peer_v[li])
    keep = min(WINDOW, win_all.shape[1])
    return rmsnorm(h, norm_out_g), jnp.stack(conv_states), new_cmp, new_slc, win_all[:, -keep:]


def setup_inputs(seed: int = 0) -> dict:
    key = jax.random.key(seed)
    ks = iter(jax.random.split(key, 32))
    f32 = jnp.float32

    def nrm(shape, scale):
        return jax.random.normal(next(ks), shape, f32) * scale

    n_pages = PAST_LEN // PAGE_SIZE
    n_used = DEC_BATCH * n_pages
    n_phys = (5 * n_used + 3) // 4
    win_buf = min(WINDOW, PAST_LEN)
    kvh = (2, N_KV_HEADS, HEAD_DIM)
    page_table = jax.random.permutation(next(ks), n_phys)[:n_used].reshape(DEC_BATCH, n_pages).astype(jnp.int32)
    d = D_MODEL
    return {
        'x_prompt': nrm((BATCH, SEQ, d), 1.0),
        'x_sample': nrm((DEC_BATCH, DEC_SEQ, d), 1.0),
        'state_conv': nrm((N_A_LAYERS, DEC_BATCH, CONV_WIDTH - 1, d), 0.5),
        'cache_cmp_kv': nrm((n_phys, PAGE_SIZE) + kvh, 1.0),
        'cache_slc_kv': nrm((n_phys, PAGE_SIZE) + kvh, 1.0),
        'cache_win_kv': nrm((DEC_BATCH, win_buf) + kvh, 1.0),
        'page_table': page_table,
        'norm_mix_g': 1.0 + nrm((DEPTH, d), 0.02),
        'norm_ffn_g': 1.0 + nrm((DEPTH, d), 0.02),
        'norm_kv_g': 1.0 + nrm((d,), 0.02),
        'norm_out_g': 1.0 + nrm((d,), 0.02),
        'conv_w_in': nrm((N_A_LAYERS, d, 2 * d), d ** -0.5),
        'conv_dw_k': nrm((N_A_LAYERS, CONV_WIDTH, d), CONV_WIDTH ** -0.5),
        'conv_dw_b': nrm((N_A_LAYERS, d), 0.02),
        'conv_ln_g': 1.0 + nrm((N_A_LAYERS, d), 0.02),
        'conv_ln_b': nrm((N_A_LAYERS, d), 0.02),
        'conv_w_out': nrm((N_A_LAYERS, d, d), d ** -0.5),
        'nsa_w_kv': nrm((d, N_BRANCH * 2 * N_KV_HEADS * HEAD_DIM), d ** -0.5),
        'nsa_cmp_pe': nrm((2, CMP_BLOCK, HEAD_DIM), 0.1),
        'nsa_cmp_w1': nrm((2, CMP_BLOCK * HEAD_DIM, CMP_HIDDEN), (CMP_BLOCK * HEAD_DIM) ** -0.5),
        'nsa_cmp_w2': nrm((2, CMP_HIDDEN, HEAD_DIM), CMP_HIDDEN ** -0.5),
        'nsa_w_in': nrm((N_B_LAYERS, d, N_HEADS * HEAD_DIM + N_HEADS * N_BRANCH), d ** -0.5),
        'nsa_w_o': nrm((N_B_LAYERS, N_HEADS * HEAD_DIM, d), (N_HEADS * HEAD_DIM) ** -0.5),
        'peer_w_q': nrm((DEPTH, d, PEER_HEADS * PEER_DKEY), d ** -0.5),
        'peer_sub_k': nrm((DEPTH, 2, PEER_KEYS, PEER_DKEY // 2), (PEER_DKEY // 2) ** -0.5),
        'peer_u': nrm((DEPTH, PEER_EXPERTS, d), d ** -0.5),
        'peer_v': nrm((DEPTH, PEER_EXPERTS, d), PEER_HEADS ** -0.5),
    }


def reference(x_prompt, x_sample, state_conv, cache_cmp_kv, cache_slc_kv, cache_win_kv, page_table,
              norm_mix_g, norm_ffn_g, norm_kv_g, norm_out_g,
              conv_w_in, conv_dw_k, conv_dw_b, conv_ln_g, conv_ln_b, conv_w_out,
              nsa_w_kv, nsa_cmp_pe, nsa_cmp_w1, nsa_cmp_w2, nsa_w_in, nsa_w_o,
              peer_w_q, peer_sub_k, peer_u, peer_v):
    params = (norm_mix_g, norm_ffn_g, norm_kv_g, norm_out_g,
              conv_w_in, conv_dw_k, conv_dw_b, conv_ln_g, conv_ln_b, conv_w_out,
              nsa_w_kv, nsa_cmp_pe, nsa_cmp_w1, nsa_cmp_w2, nsa_w_in, nsa_w_o,
              peer_w_q, peer_sub_k, peer_u, peer_v)
    bsz = x_prompt.shape[0]
    empty = jnp.zeros((bsz, 0, 2, N_KV_HEADS, HEAD_DIM), x_prompt.dtype)
    conv0 = jnp.zeros((N_A_LAYERS, bsz, CONV_WIDTH - 1, D_MODEL), x_prompt.dtype)
    y_prompt, conv_p, cmp_p, slc_p, win_p = trunk(x_prompt, conv0, empty, empty, empty, 0, *params)
    dec_b, n_pages = page_table.shape
    past_len = n_pages * PAGE_SIZE
    cmp_past = cache_cmp_kv[page_table].reshape(dec_b, past_len, 2, N_KV_HEADS, HEAD_DIM)
    slc_past = cache_slc_kv[page_table].reshape(dec_b, past_len, 2, N_KV_HEADS, HEAD_DIM)
    y_sample, conv_s, cmp_s, slc_s, win_s = trunk(x_sample, state_conv, cmp_past, slc_past, cache_win_kv,
                                                  past_len, *params)
    return (y_prompt, y_sample, conv_p, cmp_p, slc_p, win_p, conv_s, cmp_s, slc_s, win_s)
```

```python
import functools
import math

import numpy as np
import jax
import jax.numpy as jnp
from jax import lax
from jax.experimental import pallas as pl
from jax.experimental.pallas import tpu as pltpu

F32 = jnp.float32
BF16 = jnp.bfloat16

D_MODEL = 1024
EPS = 1e-6
CONV_WIDTH = 31
CONV_HALO = 32
N_HEADS = 16
HEAD_DIM = 64
N_KV_HEADS = 4
HEADS_PER_KV = 4
N_BRANCH = 3
KV_COLS = 2 * N_KV_HEADS * HEAD_DIM
K_COLS = N_KV_HEADS * HEAD_DIM
CMP_BLOCK = 32
CMP_STRIDE = 16
CMP_HIDDEN = 128
SLC_BLOCK = 64
N_SELECT = 16
WINDOW = 512
PAGE_SIZE = 128
PEER_HEADS = 8
PEER_KEYS = 128
PEER_TOPK = 16
PEER_DKEY = 256
GATE_COLS = 128

VMEM_LIMIT_BYTES = 56 * 1024 * 1024
NEG_BIG = -1e30


def _params(*sem):
    return pltpu.CompilerParams(dimension_semantics=sem, vmem_limit_bytes=VMEM_LIMIT_BYTES)


def _gelu_tanh(x):
    return 0.5 * x * (1.0 + jnp.tanh(0.7978845608028654 * (x + 0.044715 * (x * x * x))))


def _sigmoid(x):
    return 1.0 / (1.0 + jnp.exp(-x))


def _rms(x, g):
    return x * lax.rsqrt(jnp.mean(x * x, axis=-1, keepdims=True) + EPS) * g


def _norm_matmul_kernel(x_ref, g_ref, w_ref, *out_refs, mode):
    xn = _rms(x_ref[...], g_ref[...])
    xb = xn.astype(BF16)
    y = jnp.dot(xb, w_ref[...], preferred_element_type=F32)
    if mode == "glu":
        out_refs[0][...] = y[:, :D_MODEL] * _sigmoid(y[:, D_MODEL:])
    elif mode == "peer":
        out_refs[0][...] = y
        out_refs[1][...] = xb
    elif mode == "kv":
        out_refs[0][...] = y
        out_refs[1][...] = y.astype(BF16)
    elif mode == "nsa":
        out_refs[0][...] = y[:, :D_MODEL] * (HEAD_DIM ** -0.5)
        out_refs[1][...] = _sigmoid(y[:, D_MODEL:])
    else:
        raise ValueError(mode)


def _norm_matmul(x, g, w_bf16, mode):
    n = x.shape[0]
    tm = min(512, n)
    assert n % tm == 0
    ncol = w_bf16.shape[1]
    if mode == "glu":
        outs = [(D_MODEL, F32)]
    elif mode == "peer":
        outs = [(ncol, F32), (D_MODEL, BF16)]
    elif mode == "kv":
        outs = [(ncol, F32), (ncol, BF16)]
    elif mode == "nsa":
        outs = [(D_MODEL, F32), (GATE_COLS, F32)]
    res = pl.pallas_call(
        functools.partial(_norm_matmul_kernel, mode=mode),
        grid=(n // tm,),
        in_specs=[pl.BlockSpec((tm, D_MODEL), lambda i: (i, 0)),
                  pl.BlockSpec((1, D_MODEL), lambda i: (0, 0)),
                  pl.BlockSpec((D_MODEL, ncol), lambda i: (0, 0))],
        out_specs=[pl.BlockSpec((tm, c), lambda i: (i, 0)) for c, _ in outs],
        out_shape=[jax.ShapeDtypeStruct((n, c), dt) for c, dt in outs],
        compiler_params=_params("parallel"),
        name="norm_matmul_" + mode,
    )(x, g.reshape(1, D_MODEL), w_bf16)
    return res


def _matmul_res_kernel(a_ref, w_ref, h_ref, o_ref):
    o_ref[...] = h_ref[...] + jnp.dot(a_ref[...].astype(BF16), w_ref[...], preferred_element_type=F32)


def _matmul_res(a, w_bf16, h):
    n = h.shape[0]
    tm = min(512, n)
    return pl.pallas_call(
        _matmul_res_kernel,
        grid=(n // tm,),
        in_specs=[pl.BlockSpec((tm, D_MODEL), lambda i: (i, 0)),
                  pl.BlockSpec((D_MODEL, D_MODEL), lambda i: (0, 0)),
                  pl.BlockSpec((tm, D_MODEL), lambda i: (i, 0))],
        out_specs=pl.BlockSpec((tm, D_MODEL), lambda i: (i, 0)),
        out_shape=jax.ShapeDtypeStruct((n, D_MODEL), F32),
        compiler_params=_params("parallel"),
        name="attn_out_proj",
    )(a, w_bf16, h)


def _conv_kernel(prev_ref, u_ref, h_ref, dwk_ref, dwb_ref, lng_ref, lnb_ref, wout_ref, o_ref,
                 win_ref, z_ref, *, tt, chunk):
    win_ref[0:CONV_HALO, :] = prev_ref[0, 0]
    win_ref[CONV_HALO:CONV_HALO + tt, :] = u_ref[0]
    first_tap = CONV_HALO - (CONV_WIDTH - 1)

    for r0 in range(0, tt, chunk):
        acc = jnp.zeros((chunk, D_MODEL), F32)
        for w in range(CONV_WIDTH):
            acc = acc + win_ref[r0 + first_tap + w:r0 + first_tap + w + chunk, :] * dwk_ref[w:w + 1, :]
        y = acc + dwb_ref[...]
        mu = jnp.mean(y, axis=-1, keepdims=True)
        yc = y - mu
        var = jnp.mean(yc * yc, axis=-1, keepdims=True)
        yn = yc * lax.rsqrt(var + EPS) * lng_ref[...] + lnb_ref[...]
        z_ref[r0:r0 + chunk, :] = yn * _sigmoid(yn)
    o_ref[0] = h_ref[0] + jnp.dot(z_ref[...].astype(BF16), wout_ref[...], preferred_element_type=F32)


def _conv_module(u, past, h, dw_k, dw_b, ln_g, ln_b, w_out_bf16):
    b, t, _ = u.shape
    tt = min(256, t)
    chunk = min(32, tt)
    nt = t // tt
    full = jnp.concatenate([jnp.zeros((b, CONV_HALO - (CONV_WIDTH - 1), D_MODEL), F32), past, u], axis=1)
    starts = np.arange(nt) * tt
    prev = jnp.stack([full[:, s:s + CONV_HALO] for s in starts], axis=1)
    vec = lambda a: a.reshape(1, D_MODEL)
    return pl.pallas_call(
        functools.partial(_conv_kernel, tt=tt, chunk=chunk),
        grid=(b, nt),
        in_specs=[pl.BlockSpec((1, 1, CONV_HALO, D_MODEL), lambda i, j: (i, j, 0, 0)),
                  pl.BlockSpec((1, tt, D_MODEL), lambda i, j: (i, j, 0)),
                  pl.BlockSpec((1, tt, D_MODEL), lambda i, j: (i, j, 0)),
                  pl.BlockSpec((CONV_WIDTH, D_MODEL), lambda i, j: (0, 0)),
                  pl.BlockSpec((1, D_MODEL), lambda i, j: (0, 0)),
                  pl.BlockSpec((1, D_MODEL), lambda i, j: (0, 0)),
                  pl.BlockSpec((1, D_MODEL), lambda i, j: (0, 0)),
                  pl.BlockSpec((D_MODEL, D_MODEL), lambda i, j: (0, 0))],
        out_specs=pl.BlockSpec((1, tt, D_MODEL), lambda i, j: (i, j, 0)),
        out_shape=jax.ShapeDtypeStruct((b, t, D_MODEL), F32),
        scratch_shapes=[pltpu.VMEM((CONV_HALO + tt, D_MODEL), F32), pltpu.VMEM((tt, D_MODEL), F32)],
        compiler_params=_params("parallel", "parallel"),
        name="conformer_conv",
    )(prev, u, h, dw_k, vec(dw_b), vec(ln_g), vec(ln_b), w_out_bf16)


def _extract_top(s, k, rows_ref=None):
    n = s.shape[0]
    iota = lax.broadcasted_iota(jnp.int32, s.shape, 0)
    vals = []
    for r in range(k):
        m = jnp.max(s, axis=0, keepdims=True)
        idx = jnp.min(jnp.where(s == m, iota, n), axis=0, keepdims=True)
        s = jnp.where(iota == idx, -jnp.inf, s)
        vals.append(m)
        if rows_ref is not None:
            rows_ref[r:r + 1, :] = m
    return vals


def _peer_topk_kernel(q_ref, k1_ref, k2_ref, s1_ref, s2_ref, e1_ref, e2_ref, tau_ref, v2_ref):
    half = PEER_DKEY // 2
    q = q_ref[...]
    nt = (((1,), (1,)), ((), ()))
    s1 = lax.dot_general(k1_ref[...], q[:, :half], nt, preferred_element_type=F32)
    s2 = lax.dot_general(k2_ref[...], q[:, half:], nt, preferred_element_type=F32)
    v1 = _extract_top(s1, PEER_TOPK)
    v2 = _extract_top(s2, PEER_TOPK, v2_ref)
    v2_all = v2_ref[...]
    v2_top8 = v2_all[:8]
    row8 = lax.broadcasted_iota(jnp.int32, v2_top8.shape, 0)
    cands = [v1[0] + v2_all]
    for a in range(1, PEER_TOPK):
        nb = PEER_TOPK // (a + 1)
        cands.append(jnp.where(row8 < nb, v1[a] + v2_top8, -jnp.inf))
    cand = jnp.concatenate(cands, axis=0)
    best = _extract_top(cand, PEER_TOPK)
    cmax, tau = best[0], best[-1]
    z = jnp.sum(jnp.where(cand >= tau, jnp.exp(cand - cmax), 0.0), axis=0, keepdims=True)
    s1_ref[0] = s1
    s2_ref[0] = s2
    e1_ref[0] = jnp.exp(s1 - v1[0]) / z
    e2_ref[0] = jnp.exp(s2 - v2[0])
    tau_ref[0] = tau


def _peer_topk(q, sub_k):
    n = q.shape[0]
    tt = 256
    assert n % tt == 0
    big = pl.BlockSpec((1, PEER_KEYS, tt), lambda i, h: (h, 0, i))
    return pl.pallas_call(
        _peer_topk_kernel,
        grid=(n // tt, PEER_HEADS),
        in_specs=[pl.BlockSpec((tt, PEER_DKEY), lambda i, h: (i, h)),
                  pl.BlockSpec((PEER_KEYS, PEER_DKEY // 2), lambda i, h: (0, 0)),
                  pl.BlockSpec((PEER_KEYS, PEER_DKEY // 2), lambda i, h: (0, 0))],
        out_specs=[big, big, big, big, pl.BlockSpec((1, 1, tt), lambda i, h: (h, 0, i))],
        out_shape=[jax.ShapeDtypeStruct((PEER_HEADS, PEER_KEYS, n), F32)] * 4
        + [jax.ShapeDtypeStruct((PEER_HEADS, 1, n), F32)],
        scratch_shapes=[pltpu.VMEM((PEER_TOPK, tt), F32)],
        compiler_params=_params("parallel", "parallel"),
        name="peer_topk",
    )(q, sub_k[0], sub_k[1])


def _peer_dense_kernel(xn_ref, u_ref, vt_ref, s1_ref, e1_ref, s2_ref, e2_ref, tau_ref, h_ref, gout_ref,
                       o_ref, a_ref, w_ref, acc_ref, *, tt, n_i1, final_norm):
    e = pl.program_id(1)

    @pl.when(e == 0)
    def _():
        acc_ref[...] = jnp.zeros_like(acc_ref)

    nt = (((1,), (1,)), ((), ()))
    a_ref[...] = lax.dot_general(u_ref[...], xn_ref[...], nt, preferred_element_type=F32)
    n_lg = tt // 128

    def body(lg, carry):
        lanes = pl.ds(pl.multiple_of(lg * 128, 128), 128)
        for il in range(n_i1):
            rows = slice(il * PEER_KEYS, (il + 1) * PEER_KEYS)
            gate = jnp.zeros((PEER_KEYS, 128), F32)
            for hd in range(PEER_HEADS):
                s1b = s1_ref[hd, il:il + 1, lanes]
                e1b = e1_ref[hd, il:il + 1, lanes]
                taub = tau_ref[hd, :, lanes]
                s2 = s2_ref[hd, :, lanes]
                e2 = e2_ref[hd, :, lanes]
                gate = gate + jnp.where((s1b + s2) >= taub, e1b * e2, 0.0)
            w_ref[rows, lanes] = (_gelu_tanh(a_ref[rows, lanes]) * gate).astype(BF16)
        return carry

    lax.fori_loop(0, n_lg, body, 0)
    acc_ref[...] += jnp.dot(vt_ref[...], w_ref[...], preferred_element_type=F32)

    @pl.when(e == pl.num_programs(1) - 1)
    def _():
        out = h_ref[...] + acc_ref[...].T
        if final_norm:
            out = _rms(out, gout_ref[...])
        o_ref[...] = out


def _peer_dense(xn_bf16, u_bf16, vt_bf16, s1, e1, s2, e2, tau, h, g_out, final_norm):
    n = h.shape[0]
    tt = min(512, n)
    n_i1 = 8
    et = n_i1 * PEER_KEYS
    n_exp = u_bf16.shape[0]
    tok = lambda i, e: (i, 0)
    return pl.pallas_call(
        functools.partial(_peer_dense_kernel, tt=tt, n_i1=n_i1, final_norm=final_norm),
        grid=(n // tt, n_exp // et),
        in_specs=[pl.BlockSpec((tt, D_MODEL), tok),
                  pl.BlockSpec((et, D_MODEL), lambda i, e: (e, 0)),
                  pl.BlockSpec((D_MODEL, et), lambda i, e: (0, e)),
                  pl.BlockSpec((PEER_HEADS, n_i1, tt), lambda i, e: (0, e, i)),
                  pl.BlockSpec((PEER_HEADS, n_i1, tt), lambda i, e: (0, e, i)),
                  pl.BlockSpec((PEER_HEADS, PEER_KEYS, tt), lambda i, e: (0, 0, i)),
                  pl.BlockSpec((PEER_HEADS, PEER_KEYS, tt), lambda i, e: (0, 0, i)),
                  pl.BlockSpec((PEER_HEADS, 1, tt), lambda i, e: (0, 0, i)),
                  pl.BlockSpec((tt, D_MODEL), tok),
                  pl.BlockSpec((1, D_MODEL), lambda i, e: (0, 0))],
        out_specs=pl.BlockSpec((tt, D_MODEL), tok),
        out_shape=jax.ShapeDtypeStruct((n, D_MODEL), F32),
        scratch_shapes=[pltpu.VMEM((et, tt), F32), pltpu.VMEM((et, tt), BF16), pltpu.VMEM((D_MODEL, tt), F32)],
        compiler_params=_params("parallel", "arbitrary"),
        name="peer_dense",
    )(xn_bf16, u_bf16, vt_bf16, s1, e1, s2, e2, tau, h, g_out.reshape(1, D_MODEL))


def _peer_layer(h, g_norm, w_q, sub_k, u_emb, v_emb, g_out, final_norm):
    q, xn = _norm_matmul(h, g_norm, w_q.astype(BF16), "peer")
    s1, s2, e1, e2, tau = _peer_topk(q, sub_k)
    return _peer_dense(xn, u_emb.astype(BF16), v_emb.T.astype(BF16), s1, e1, s2, e2, tau, h, g_out, final_norm)


def _gather_kernel(pt_ref, page_ref, new_ref, o_ref, *, n_pages):
    p = pl.program_id(1)

    @pl.when(p < n_pages)
    def _():
        o_ref[0] = page_ref[0].astype(o_ref.dtype)

    @pl.when(p >= n_pages)
    def _():
        o_ref[0] = new_ref[0].astype(o_ref.dtype)


def _gather_pages(cache, page_table, new_rows, tail, out_dtype):
    b, n_pages = page_table.shape
    n_tail = tail // PAGE_SIZE
    new_pad = jnp.pad(new_rows, ((0, 0), (0, tail - new_rows.shape[1]), (0, 0)))
    grid_spec = pltpu.PrefetchScalarGridSpec(
        num_scalar_prefetch=1,
        grid=(b, n_pages + n_tail),
        in_specs=[pl.BlockSpec((1, PAGE_SIZE, KV_COLS),
                               lambda i, p, pt: (pt[i, jnp.minimum(p, n_pages - 1)], 0, 0)),
                  pl.BlockSpec((1, PAGE_SIZE, KV_COLS),
                               lambda i, p, pt: (i, jnp.maximum(p - n_pages, 0), 0))],
        out_specs=pl.BlockSpec((1, PAGE_SIZE, KV_COLS), lambda i, p, pt: (i, p, 0)),
    )
    return pl.pallas_call(
        functools.partial(_gather_kernel, n_pages=n_pages),
        grid_spec=grid_spec,
        out_shape=jax.ShapeDtypeStruct((b, (n_pages + n_tail) * PAGE_SIZE, KV_COLS), out_dtype),
        compiler_params=_params("parallel", "arbitrary"),
        name="gather_pages",
    )(page_table, cache, new_pad)


CMP_TILE = 128


def _compress_kernel(main_ref, halo_ref, pe_ref, w1k_ref, w1v_ref, w2k_ref, w2v_ref, o_ref, win_ref):
    n_tok = CMP_TILE * CMP_STRIDE
    n_lane_tiles = KV_COLS // 128
    for c in range(n_lane_tiles):
        win_ref[c, 0:n_tok, :] = main_ref[0, :, c * 128:(c + 1) * 128]
        win_ref[c, n_tok:n_tok + CMP_STRIDE, :] = halo_ref[0, :, c * 128:(c + 1) * 128]
    hk = jnp.zeros((CMP_TILE, N_KV_HEADS * CMP_HIDDEN), F32)
    hv = jnp.zeros((CMP_TILE, N_KV_HEADS * CMP_HIDDEN), F32)
    for s in range(CMP_BLOCK):
        x = jnp.concatenate([win_ref[c, pl.ds(s, CMP_TILE, stride=CMP_STRIDE), :] for c in range(n_lane_tiles)],
                            axis=1)
        x = (x + pe_ref[s]).astype(BF16)
        hk = hk + jnp.dot(x[:, :K_COLS], w1k_ref[s], preferred_element_type=F32)
        hv = hv + jnp.dot(x[:, K_COLS:], w1v_ref[s], preferred_element_type=F32)
    o_ref[0, :, :K_COLS] = jnp.dot(_gelu_tanh(hk).astype(BF16), w2k_ref[...], preferred_element_type=F32)
    o_ref[0, :, K_COLS:] = jnp.dot(_gelu_tanh(hv).astype(BF16), w2v_ref[...], preferred_element_type=F32)


def _block_diag(w):
    eye = jnp.eye(N_KV_HEADS, dtype=w.dtype)
    out = jnp.einsum("gh,...ab->...gahb", eye, w)
    return out.reshape(w.shape[:-2] + (N_KV_HEADS * w.shape[-2], N_KV_HEADS * w.shape[-1]))


def _compress(rows, n_tiles, cmp_pe, cmp_w1, cmp_w2):
    b, length = rows.shape[:2]
    n_tok = CMP_TILE * CMP_STRIDE
    last_halo = length // CMP_STRIDE - 1
    w1 = cmp_w1.reshape(2, CMP_BLOCK, HEAD_DIM, CMP_HIDDEN)
    w1k, w1v = _block_diag(w1[0]).astype(BF16), _block_diag(w1[1]).astype(BF16)
    w2k, w2v = _block_diag(cmp_w2[0]).astype(BF16), _block_diag(cmp_w2[1]).astype(BF16)
    pe = jnp.concatenate([jnp.tile(cmp_pe[0], (1, N_KV_HEADS)), jnp.tile(cmp_pe[1], (1, N_KV_HEADS))], axis=1)
    pe = pe.reshape(CMP_BLOCK, 1, KV_COLS)
    const3 = lambda i, j: (0, 0, 0)
    return pl.pallas_call(
        _compress_kernel,
        grid=(b, n_tiles),
        in_specs=[pl.BlockSpec((1, n_tok, KV_COLS), lambda i, j: (i, j, 0)),
                  pl.BlockSpec((1, CMP_STRIDE, KV_COLS),
                               lambda i, j: (i, jnp.minimum((j + 1) * CMP_TILE, last_halo), 0)),
                  pl.BlockSpec((CMP_BLOCK, 1, KV_COLS), const3),
                  pl.BlockSpec((CMP_BLOCK, K_COLS, N_KV_HEADS * CMP_HIDDEN), const3),
                  pl.BlockSpec((CMP_BLOCK, K_COLS, N_KV_HEADS * CMP_HIDDEN), const3),
                  pl.BlockSpec((N_KV_HEADS * CMP_HIDDEN, K_COLS), lambda i, j: (0, 0)),
                  pl.BlockSpec((N_KV_HEADS * CMP_HIDDEN, K_COLS), lambda i, j: (0, 0))],
        out_specs=pl.BlockSpec((1, CMP_TILE, KV_COLS), lambda i, j: (i, j, 0)),
        out_shape=jax.ShapeDtypeStruct((b, n_tiles * CMP_TILE, KV_COLS), F32),
        scratch_shapes=[pltpu.VMEM((KV_COLS // 128, n_tok + CMP_STRIDE, 128), F32)],
        compiler_params=_params("parallel", "parallel"),
        name="cmp_compress",
    )(rows, rows, pe, w1k, w1v, w2k, w2v)


def _head_perm():
    c = np.arange(D_MODEL)
    h, g, d = c // K_COLS, (c % K_COLS) // HEAD_DIM, c % HEAD_DIM
    return (g * HEADS_PER_KV + h) * HEAD_DIM + d


def _gate_expand(branch):
    c = np.arange(D_MODEL)
    h, g = c // K_COLS, (c % K_COLS) // HEAD_DIM
    m = np.zeros((GATE_COLS, D_MODEL), np.float32)
    m[(g * HEADS_PER_KV + h) * N_BRANCH + branch, c] = 1.0
    return jnp.asarray(m)


def _block_diag_queries(q):
    tq = q.shape[0]
    lane_g = lax.broadcasted_iota(jnp.int32, (tq, K_COLS), 1) // HEAD_DIM
    blocks = []
    for g in range(N_KV_HEADS):
        for h in range(HEADS_PER_KV):
            blocks.append(jnp.where(lane_g == g, q[:, h * K_COLS:(h + 1) * K_COLS], 0.0))
    return jnp.concatenate(blocks, axis=0)


def _diag_heads(acc, tq):
    lane_g = lax.broadcasted_iota(jnp.int32, (tq, K_COLS), 1) // HEAD_DIM
    outs = []
    for h in range(HEADS_PER_KV):
        o = jnp.zeros((tq, K_COLS), F32)
        for g in range(N_KV_HEADS):
            r0 = (g * HEADS_PER_KV + h) * tq
            o = o + jnp.where(lane_g == g, acc[r0:r0 + tq], 0.0)
        outs.append(o)
    return jnp.concatenate(outs, axis=1)


_NT = (((1,), (1,)), ((), ()))


def _cmp_attn_kernel(q_ref, gate_ref, kvc_ref, ovt_ref, eg_ref, oc_ref, sel_ref, sc_ref,
                     *, tq, pos0, n_slc_loop):
    q0 = pos0 + pl.program_id(1) * tq
    n_pad = kvc_ref.shape[1]
    s_pad = ovt_ref.shape[0]
    rows = N_HEADS * tq
    qbd = _block_diag_queries(q_ref[0]).astype(BF16)
    kc = kvc_ref[0, :, :K_COLS].astype(BF16)
    vc = kvc_ref[0, :, K_COLS:].astype(BF16)
    s = lax.dot_general(qbd, kc, _NT, preferred_element_type=F32)
    t_row = q0 + (lax.broadcasted_iota(jnp.int32, (rows, n_pad), 0) & (tq - 1))
    n_col = lax.broadcasted_iota(jnp.int32, (rows, n_pad), 1)
    vis = (n_col * CMP_STRIDE + (CMP_BLOCK - 1)) <= t_row
    m = jnp.max(jnp.where(vis, s, -jnp.inf), axis=-1, keepdims=True)
    m = jnp.where(m > -jnp.inf, m, 0.0)
    e = jnp.where(vis, jnp.exp(s - m), 0.0)
    p = e / jnp.maximum(jnp.sum(e, axis=-1, keepdims=True), 1e-30)
    oc = _diag_heads(jnp.dot(p.astype(BF16), vc, preferred_element_type=F32), tq)
    oc_ref[0] = oc * jnp.dot(gate_ref[0], eg_ref[...], preferred_element_type=F32)

    psum = jnp.concatenate(
        [sum(p[(g * HEADS_PER_KV + h) * tq:(g * HEADS_PER_KV + h + 1) * tq] for h in range(HEADS_PER_KV))
         for g in range(N_KV_HEADS)], axis=0)
    imp_t = lax.dot_general(ovt_ref[...], psum, _NT, preferred_element_type=F32)
    cols = N_KV_HEADS * tq
    blk = lax.broadcasted_iota(jnp.int32, (s_pad, cols), 0)
    cur = (q0 + (lax.broadcasted_iota(jnp.int32, (s_pad, cols), 1) & (tq - 1))) // SLC_BLOCK
    forced = (blk == 0) | (blk == cur) | (blk == cur - 1)
    score = jnp.where(forced, jnp.inf, jnp.where(blk <= cur, imp_t, -jnp.inf))
    sc_ref[...] = score

    def rank_body(jb, rank):
        rows8 = sc_ref[pl.ds(pl.multiple_of(jb * 8, 8), 8), :]
        for r in range(8):
            row = rows8[r:r + 1]
            tie = jnp.where(row == score, jnp.where(jb * 8 + r < blk, 1.0, 0.0), 0.0)
            rank = rank + jnp.where(row > score, 1.0, tie)
        return rank

    rank = lax.fori_loop(0, n_slc_loop // 8, rank_body, jnp.zeros((s_pad, cols), F32))
    sel_t = jnp.where(rank < float(N_SELECT), 1.0, 0.0).astype(BF16)
    ci = lax.broadcasted_iota(jnp.int32, (cols, cols), 0)
    cj = lax.broadcasted_iota(jnp.int32, (cols, cols), 1)
    eye = jnp.where(ci == cj, 1.0, 0.0).astype(BF16)
    sel_ref[0, 0] = lax.dot_general(eye, sel_t, _NT, preferred_element_type=F32)


def _cmp_attn(q, gates, kvc, ov_t, pos0, tq, n_slc):
    b, t, _ = q.shape
    s_pad = ov_t.shape[0]
    n_pad = kvc.shape[1]
    n_slc_loop = -(-n_slc // 8) * 8
    return pl.pallas_call(
        functools.partial(_cmp_attn_kernel, tq=tq, pos0=pos0, n_slc_loop=n_slc_loop),
        grid=(b, t // tq),
        in_specs=[pl.BlockSpec((1, tq, D_MODEL), lambda i, j: (i, j, 0)),
                  pl.BlockSpec((1, tq, GATE_COLS), lambda i, j: (i, j, 0)),
                  pl.BlockSpec((1, n_pad, KV_COLS), lambda i, j: (i, 0, 0)),
                  pl.BlockSpec((s_pad, n_pad), lambda i, j: (0, 0)),
                  pl.BlockSpec((GATE_COLS, D_MODEL), lambda i, j: (0, 0))],
        out_specs=[pl.BlockSpec((1, tq, D_MODEL), lambda i, j: (i, j, 0)),
                   pl.BlockSpec((1, 1, N_KV_HEADS * tq, s_pad), lambda i, j: (i, j, 0, 0))],
        out_shape=[jax.ShapeDtypeStruct((b, t, D_MODEL), F32),
                   jax.ShapeDtypeStruct((b, t // tq, N_KV_HEADS * tq, s_pad), F32)],
        scratch_shapes=[pltpu.VMEM((s_pad, N_KV_HEADS * tq), F32)],
        compiler_params=_params("parallel", "parallel"),
        name="nsa_cmp_select",
    )(q, gates, kvc, ov_t, _gate_expand(0))


def _sparse_attn_kernel(q_ref, gate_ref, sel_ref, oc_ref, kslc_ref, kwin_ref, egs_ref, egw_ref, o_ref,
                        qbd_ref, m_ref, l_ref, acc_ref, *, tq, tk, pos0, win_pos0):
    q0 = pos0 + pl.program_id(1) * tq
    rows = N_HEADS * tq
    grows = N_KV_HEADS * tq
    s_pad = sel_ref.shape[3]
    qbd_ref[...] = _block_diag_queries(q_ref[0]).astype(BF16)
    t_all = q0 + (lax.broadcasted_iota(jnp.int32, (rows, tk), 0) & (tq - 1))
    k_col = lax.broadcasted_iota(jnp.int32, (rows, tk), 1)

    def run(k_ref, lo, hi, mask_fn):
        m_ref[...] = jnp.full_like(m_ref, NEG_BIG)
        l_ref[...] = jnp.zeros_like(l_ref)
        acc_ref[...] = jnp.zeros_like(acc_ref)

        def body(j, carry):
            ks = pl.ds(pl.multiple_of(j * tk, tk), tk)
            kt = k_ref[0, ks, 0:K_COLS]
            vt = k_ref[0, ks, K_COLS:KV_COLS]
            s = lax.dot_general(qbd_ref[...], kt, _NT, preferred_element_type=F32)
            mask = mask_fn(j)
            s = jnp.where(mask, s, NEG_BIG)
            m_old = m_ref[...]
            m_new = jnp.maximum(m_old, jnp.max(s, axis=-1, keepdims=True))
            p = jnp.where(mask, jnp.exp(s - m_new), 0.0)
            alpha = jnp.exp(m_old - m_new)
            l_ref[...] = alpha * l_ref[...] + jnp.sum(p, axis=-1, keepdims=True)
            acc_ref[...] = alpha * acc_ref[...] + jnp.dot(p.astype(BF16), vt, preferred_element_type=F32)
            m_ref[...] = m_new
            return carry

        lax.fori_loop(lo, hi, body, 0)
        return _diag_heads(acc_ref[...] / jnp.maximum(l_ref[...], 1e-30), tq)

    sel = sel_ref[0, 0].astype(BF16)
    blk_row = lax.broadcasted_iota(jnp.int32, (s_pad, tk), 0)
    blk_col = lax.broadcasted_iota(jnp.int32, (s_pad, tk), 1) // SLC_BLOCK

    def slc_mask(j):
        expand = jnp.where(blk_row == j * (tk // SLC_BLOCK) + blk_col, 1.0, 0.0).astype(BF16)
        chosen = jnp.dot(sel, expand, preferred_element_type=F32)
        per_head = jnp.concatenate(
            [chosen[g * tq:(g + 1) * tq] for g in range(N_KV_HEADS) for _ in range(HEADS_PER_KV)], axis=0)
        return (per_head > 0.5) & (j * tk + k_col <= t_all)

    def win_mask(j):
        kpos = win_pos0 + j * tk + k_col
        return (kpos <= t_all) & (kpos > t_all - WINDOW)

    t_last = q0 + tq - 1
    o_s = run(kslc_ref, 0, t_last // tk + 1, slc_mask)
    w_lo = jnp.maximum(q0 - (WINDOW - 1) - win_pos0, 0) // tk
    o_w = run(kwin_ref, w_lo, (t_last - win_pos0) // tk + 1, win_mask)
    gate = gate_ref[0]
    g_s = jnp.dot(gate, egs_ref[...], preferred_element_type=F32)
    g_w = jnp.dot(gate, egw_ref[...], preferred_element_type=F32)
    o_ref[0] = (oc_ref[0] + g_s * o_s + g_w * o_w).astype(o_ref.dtype)


def _sparse_attn(q, gates, sel, oc, kslc, kwin, slc_col, win_col, pos0, win_pos0, tq, tk):
    b, t, _ = q.shape
    s_pad = sel.shape[3]
    ls, lw = kslc.shape[1], kwin.shape[1]
    assert ls % tk == 0 and lw % tk == 0
    rows = N_HEADS * tq
    qtile = lambda i, j: (i, j, 0)
    return pl.pallas_call(
        functools.partial(_sparse_attn_kernel, tq=tq, tk=tk, pos0=pos0, win_pos0=win_pos0),
        grid=(b, t // tq),
        in_specs=[pl.BlockSpec((1, tq, D_MODEL), qtile),
                  pl.BlockSpec((1, tq, GATE_COLS), qtile),
                  pl.BlockSpec((1, 1, N_KV_HEADS * tq, s_pad), lambda i, j: (i, j, 0, 0)),
                  pl.BlockSpec((1, tq, D_MODEL), qtile),
                  pl.BlockSpec((1, ls, KV_COLS), lambda i, j: (i, 0, slc_col)),
                  pl.BlockSpec((1, lw, KV_COLS), lambda i, j: (i, 0, win_col)),
                  pl.BlockSpec((GATE_COLS, D_MODEL), lambda i, j: (0, 0)),
                  pl.BlockSpec((GATE_COLS, D_MODEL), lambda i, j: (0, 0))],
        out_specs=pl.BlockSpec((1, tq, D_MODEL), qtile),
        out_shape=jax.ShapeDtypeStruct((b, t, D_MODEL), F32),
        scratch_shapes=[pltpu.VMEM((rows, K_COLS), BF16), pltpu.VMEM((rows, 1), F32),
                        pltpu.VMEM((rows, 1), F32), pltpu.VMEM((rows, K_COLS), F32)],
        compiler_params=_params("parallel", "arbitrary"),
        name="nsa_select_window",
    )(q, gates, sel, oc, kslc, kwin, _gate_expand(1), _gate_expand(2))


ATTN_TK = 512


def _overlap_t(s_pad, n_pad, n_slc, n_cmp):
    n = np.arange(n_pad)[None, :]
    s = np.arange(s_pad)[:, None]
    ov = ((n * CMP_STRIDE < s * SLC_BLOCK + SLC_BLOCK) & (n * CMP_STRIDE + CMP_BLOCK - 1 >= s * SLC_BLOCK)
          & (n < n_cmp) & (s < n_slc))
    return jnp.asarray(ov.astype(np.float32))


def _trunk(x, conv_past, past, pos0, p):
    b, t, _ = x.shape
    n = b * t
    kvh = (2, N_KV_HEADS, HEAD_DIM)
    h = x.reshape(n, D_MODEL)

    (u,) = _norm_matmul(h, p["norm_mix_g"][0], p["conv_w_in"][0].astype(BF16), "glu")
    u = u.reshape(b, t, D_MODEL)
    h = _conv_module(u, conv_past, h.reshape(b, t, D_MODEL), p["conv_dw_k"][0], p["conv_dw_b"][0],
                     p["conv_ln_g"][0], p["conv_ln_b"][0], p["conv_w_out"][0].astype(BF16)).reshape(n, D_MODEL)
    conv_state = jnp.concatenate([conv_past, u], axis=1)[:, -(CONV_WIDTH - 1):][None]
    h = _peer_layer(h, p["norm_ffn_g"][0], p["peer_w_q"][0], p["peer_sub_k"][0], p["peer_u"][0],
                    p["peer_v"][0], p["norm_out_g"], False)

    kv, kv_bf16 = _norm_matmul(h, p["norm_kv_g"], p["nsa_w_kv"].astype(BF16), "kv")
    kv = kv.reshape(b, t, N_BRANCH * KV_COLS)
    kv_bf16 = kv_bf16.reshape(b, t, N_BRANCH * KV_COLS)
    new_cmp, new_slc, new_win = (kv[..., i * KV_COLS:(i + 1) * KV_COLS] for i in range(N_BRANCH))
    if past is None:
        length, win_pos0 = t, pos0
        cmp_rows, k_slc, k_win, slc_col, win_col = kv, kv_bf16, kv_bf16, 1, 2
        win_all = new_win
    else:
        cache_cmp, cache_slc, cache_win, page_table = past
        length, win_pos0 = pos0 + t, pos0 - cache_win.shape[1]
        cmp_rows = _gather_pages(cache_cmp, page_table, new_cmp, PAGE_SIZE, F32)
        k_slc = _gather_pages(cache_slc, page_table, new_slc, ATTN_TK, BF16)
        win_all = jnp.concatenate([cache_win, new_win], axis=1)
        lw = -(-win_all.shape[1] // ATTN_TK) * ATTN_TK
        k_win = jnp.pad(win_all, ((0, 0), (0, lw - win_all.shape[1]), (0, 0))).astype(BF16)
        slc_col, win_col = 0, 0
    n_cmp = (length - CMP_BLOCK) // CMP_STRIDE + 1
    n_slc = -(-length // SLC_BLOCK)
    n_tiles = -(-n_cmp // CMP_TILE)
    kvc = _compress(cmp_rows, n_tiles, p["nsa_cmp_pe"], p["nsa_cmp_w1"], p["nsa_cmp_w2"])

    perm = _head_perm()
    w_in = p["nsa_w_in"][0]
    w_in = jnp.concatenate([w_in[:, perm], w_in[:, D_MODEL:],
                            jnp.zeros((D_MODEL, GATE_COLS - N_HEADS * N_BRANCH), F32)], axis=1).astype(BF16)
    q, gates = _norm_matmul(h, p["norm_mix_g"][1], w_in, "nsa")
    q = q.reshape(b, t, D_MODEL)
    gates = gates.reshape(b, t, GATE_COLS)
    tq = min(64, t)
    s_pad = -(-n_slc // 128) * 128
    oc, sel = _cmp_attn(q, gates, kvc, _overlap_t(s_pad, kvc.shape[1], n_slc, n_cmp), pos0, tq, n_slc)
    o = _sparse_attn(q, gates, sel, oc, k_slc, k_win, slc_col, win_col, pos0, win_pos0, tq, ATTN_TK)
    h = _matmul_res(o.reshape(n, D_MODEL), p["nsa_w_o"][0][perm, :].astype(BF16), h)
    y = _peer_layer(h, p["norm_ffn_g"][1], p["peer_w_q"][1], p["peer_sub_k"][1], p["peer_u"][1],
                    p["peer_v"][1], p["norm_out_g"], True)
    keep = min(WINDOW, win_all.shape[1])
    shape5 = lambda a: a.reshape(a.shape[:2] + kvh)
    return (y.reshape(b, t, D_MODEL), conv_state, shape5(new_cmp), shape5(new_slc), shape5(win_all[:, -keep:]))


def kernel(x_prompt, x_sample, state_conv, cache_cmp_kv, cache_slc_kv, cache_win_kv, page_table, norm_mix_g, norm_ffn_g, norm_kv_g, norm_out_g, conv_w_in, conv_dw_k, conv_dw_b, conv_ln_g, conv_ln_b, conv_w_out, nsa_w_kv, nsa_cmp_pe, nsa_cmp_w1, nsa_cmp_w2, nsa_w_in, nsa_w_o, peer_w_q, peer_sub_k, peer_u, peer_v):
    p = dict(norm_mix_g=norm_mix_g, norm_ffn_g=norm_ffn_g, norm_kv_g=norm_kv_g, norm_out_g=norm_out_g,
             conv_w_in=conv_w_in, conv_dw_k=conv_dw_k, conv_dw_b=conv_dw_b, conv_ln_g=conv_ln_g,
             conv_ln_b=conv_ln_b, conv_w_out=conv_w_out, nsa_w_kv=nsa_w_kv, nsa_cmp_pe=nsa_cmp_pe,
             nsa_cmp_w1=nsa_cmp_w1, nsa_cmp_w2=nsa_cmp_w2, nsa_w_in=nsa_w_in, nsa_w_o=nsa_w_o,
             peer_w_q=peer_w_q, peer_sub_k=peer_sub_k, peer_u=peer_u, peer_v=peer_v)
    bsz = x_prompt.shape[0]
    conv0 = jnp.zeros((bsz, CONV_WIDTH - 1, D_MODEL), x_prompt.dtype)
    y_p, conv_p, cmp_p, slc_p, win_p = _trunk(x_prompt, conv0, None, 0, p)
    dec_b, n_pages = page_table.shape
    flat = lambda c: c.reshape(c.shape[0], c.shape[1], KV_COLS)
    past = (flat(cache_cmp_kv), flat(cache_slc_kv), flat(cache_win_kv), page_table)
    y_s, conv_s, cmp_s, slc_s, win_s = _trunk(x_sample, state_conv[0], past, n_pages * PAGE_SIZE, p)
    return (y_p, y_s, conv_p, cmp_p, slc_p, win_p, conv_s, cmp_s, slc_s, win_s)
```

```python
import functools
import math

import numpy as np
import jax
import jax.numpy as jnp
from jax import lax
from jax.experimental import pallas as pl
from jax.experimental.pallas import tpu as pltpu

F32 = jnp.float32
BF16 = jnp.bfloat16

D_MODEL = 1024
EPS = 1e-6
CONV_WIDTH = 31
CONV_HALO = 32
N_HEADS = 16
HEAD_DIM = 64
N_KV_HEADS = 4
HEADS_PER_KV = 4
N_BRANCH = 3
KV_COLS = 2 * N_KV_HEADS * HEAD_DIM
K_COLS = N_KV_HEADS * HEAD_DIM
CMP_BLOCK = 32
CMP_STRIDE = 16
CMP_HIDDEN = 128
SLC_BLOCK = 64
N_SELECT = 16
WINDOW = 512
PAGE_SIZE = 128
PEER_HEADS = 8
PEER_KEYS = 128
PEER_TOPK = 16
PEER_DKEY = 256
GATE_COLS = 128

VMEM_LIMIT_BYTES = 56 * 1024 * 1024
NEG_BIG = -1e30


def _params(*sem):
    return pltpu.CompilerParams(dimension_semantics=sem, vmem_limit_bytes=VMEM_LIMIT_BYTES)


def _gelu_tanh(x):
    return 0.5 * x * (1.0 + jnp.tanh(0.7978845608028654 * (x + 0.044715 * (x * x * x))))


def _sigmoid(x):
    return 1.0 / (1.0 + jnp.exp(-x))


def _rms(x, g):
    return x * lax.rsqrt(jnp.mean(x * x, axis=-1, keepdims=True) + EPS) * g


def _norm_matmul_kernel(x_ref, g_ref, w_ref, *out_refs, mode):
    xn = _rms(x_ref[...], g_ref[...])
    xb = xn.astype(BF16)
    y = jnp.dot(xb, w_ref[...], preferred_element_type=F32)
    if mode == "glu":
        out_refs[0][...] = y[:, :D_MODEL] * _sigmoid(y[:, D_MODEL:])
    elif mode == "peer":
        out_refs[0][...] = y
        out_refs[1][...] = xb
    elif mode == "kv":
        out_refs[0][...] = y
        out_refs[1][...] = y.astype(BF16)
    elif mode == "nsa":
        out_refs[0][...] = y[:, :D_MODEL] * (HEAD_DIM ** -0.5)
        out_refs[1][...] = _sigmoid(y[:, D_MODEL:])
    else:
        raise ValueError(mode)


def _norm_matmul(x, g, w_bf16, mode):
    n = x.shape[0]
    tm = min(512, n)
    assert n % tm == 0
    ncol = w_bf16.shape[1]
    if mode == "glu":
        outs = [(D_MODEL, F32)]
    elif mode == "peer":
        outs = [(ncol, F32), (D_MODEL, BF16)]
    elif mode == "kv":
        outs = [(ncol, F32), (ncol, BF16)]
    elif mode == "nsa":
        outs = [(D_MODEL, F32), (GATE_COLS, F32)]
    res = pl.pallas_call(
        functools.partial(_norm_matmul_kernel, mode=mode),
        grid=(n // tm,),
        in_specs=[pl.BlockSpec((tm, D_MODEL), lambda i: (i, 0)),
                  pl.BlockSpec((1, D_MODEL), lambda i: (0, 0)),
                  pl.BlockSpec((D_MODEL, ncol), lambda i: (0, 0))],
        out_specs=[pl.BlockSpec((tm, c), lambda i: (i, 0)) for c, _ in outs],
        out_shape=[jax.ShapeDtypeStruct((n, c), dt) for c, dt in outs],
        compiler_params=_params("parallel"),
        name="norm_matmul_" + mode,
    )(x, g.reshape(1, D_MODEL), w_bf16)
    return res


def _matmul_res_kernel(a_ref, w_ref, h_ref, o_ref):
    o_ref[...] = h_ref[...] + jnp.dot(a_ref[...].astype(BF16), w_ref[...], preferred_element_type=F32)


def _matmul_res(a, w_bf16, h):
    n = h.shape[0]
    tm = min(512, n)
    return pl.pallas_call(
        _matmul_res_kernel,
        grid=(n // tm,),
        in_specs=[pl.BlockSpec((tm, D_MODEL), lambda i: (i, 0)),
                  pl.BlockSpec((D_MODEL, D_MODEL), lambda i: (0, 0)),
                  pl.BlockSpec((tm, D_MODEL), lambda i: (i, 0))],
        out_specs=pl.BlockSpec((tm, D_MODEL), lambda i: (i, 0)),
        out_shape=jax.ShapeDtypeStruct((n, D_MODEL), F32),
        compiler_params=_params("parallel"),
        name="attn_out_proj",
    )(a, w_bf16, h)


def _conv_kernel(prev_ref, u_ref, h_ref, dwk_ref, dwb_ref, lng_ref, lnb_ref, wout_ref, o_ref,
                 win_ref, z_ref, *, tt, chunk):
    win_ref[0:CONV_HALO, :] = prev_ref[0, 0]
    win_ref[CONV_HALO:CONV_HALO + tt, :] = u_ref[0]
    first_tap = CONV_HALO - (CONV_WIDTH - 1)

    for r0 in range(0, tt, chunk):
        acc = jnp.zeros((chunk, D_MODEL), F32)
        for w in range(CONV_WIDTH):
            acc = acc + win_ref[r0 + first_tap + w:r0 + first_tap + w + chunk, :] * dwk_ref[w:w + 1, :]
        y = acc + dwb_ref[...]
        mu = jnp.mean(y, axis=-1, keepdims=True)
        yc = y - mu
        var = jnp.mean(yc * yc, axis=-1, keepdims=True)
        yn = yc * lax.rsqrt(var + EPS) * lng_ref[...] + lnb_ref[...]
        z_ref[r0:r0 + chunk, :] = yn * _sigmoid(yn)
    o_ref[0] = h_ref[0] + jnp.dot(z_ref[...].astype(BF16), wout_ref[...], preferred_element_type=F32)


def _conv_module(u, past, h, dw_k, dw_b, ln_g, ln_b, w_out_bf16):
    b, t, _ = u.shape
    tt = min(256, t)
    chunk = min(32, tt)
    nt = t // tt
    full = jnp.concatenate([jnp.zeros((b, CONV_HALO - (CONV_WIDTH - 1), D_MODEL), F32), past, u], axis=1)
    starts = np.arange(nt) * tt
    prev = jnp.stack([full[:, s:s + CONV_HALO] for s in starts], axis=1)
    vec = lambda a: a.reshape(1, D_MODEL)
    return pl.pallas_call(
        functools.partial(_conv_kernel, tt=tt, chunk=chunk),
        grid=(b, nt),
        in_specs=[pl.BlockSpec((1, 1, CONV_HALO, D_MODEL), lambda i, j: (i, j, 0, 0)),
                  pl.BlockSpec((1, tt, D_MODEL), lambda i, j: (i, j, 0)),
                  pl.BlockSpec((1, tt, D_MODEL), lambda i, j: (i, j, 0)),
                  pl.BlockSpec((CONV_WIDTH, D_MODEL), lambda i, j: (0, 0)),
                  pl.BlockSpec((1, D_MODEL), lambda i, j: (0, 0)),
                  pl.BlockSpec((1, D_MODEL), lambda i, j: (0, 0)),
                  pl.BlockSpec((1, D_MODEL), lambda i, j: (0, 0)),
                  pl.BlockSpec((D_MODEL, D_MODEL), lambda i, j: (0, 0))],
        out_specs=pl.BlockSpec((1, tt, D_MODEL), lambda i, j: (i, j, 0)),
        out_shape=jax.ShapeDtypeStruct((b, t, D_MODEL), F32),
        scratch_shapes=[pltpu.VMEM((CONV_HALO + tt, D_MODEL), F32), pltpu.VMEM((tt, D_MODEL), F32)],
        compiler_params=_params("parallel", "parallel"),
        name="conformer_conv",
    )(prev, u, h, dw_k, vec(dw_b), vec(ln_g), vec(ln_b), w_out_bf16)


NOT_RETRIEVED = 127.0
BF16_ROWS = 16


def _extract_top(s, k, rows_ref=None):
    n = s.shape[0]
    iota = lax.broadcasted_iota(jnp.int32, s.shape, 0)
    rank = jnp.full(s.shape, NOT_RETRIEVED, F32)
    vals, idxs = [], []
    for r in range(k):
        m = jnp.max(s, axis=0, keepdims=True)
        idx = jnp.min(jnp.where(s == m, iota, n), axis=0, keepdims=True)
        hit = iota == idx
        s = jnp.where(hit, -jnp.inf, s)
        rank = jnp.where(hit, float(r), rank)
        vals.append(m)
        idxs.append(idx)
        if rows_ref is not None:
            rows_ref[r:r + 1, :] = m
    return vals, idxs, rank


def _bf16_bits(x):
    return pltpu.bitcast(x.astype(BF16).astype(F32), jnp.uint32)


def _bf16_pair(x):
    hi = _bf16_bits(x)
    return hi | (hi >> 16)


def _bf16_pack_rows(x):
    r, c = x.shape
    x4 = x.reshape(r // 16, 2, 8, c)
    words = _bf16_bits(x4[:, 1]) | (_bf16_bits(x4[:, 0]) >> 16)
    return words.reshape(r // 2, c)


def _bf16_unpack_rows(words):
    lo = pltpu.bitcast(words << 16, F32)
    hi = pltpu.bitcast(words & jnp.uint32(0xFFFF0000), F32)
    return jnp.stack([lo, hi], axis=1)


def _peer_topk_kernel(q_ref, k1_ref, k2_ref, n1_ref, e1_ref, r2_ref, e2_ref, v2_ref):
    half = PEER_DKEY // 2
    q = q_ref[...]
    s1 = lax.dot_general(k1_ref[...], q[:, :half], _NT, preferred_element_type=F32)
    s2 = lax.dot_general(k2_ref[...], q[:, half:], _NT, preferred_element_type=F32)
    v1, idx1, _ = _extract_top(s1, PEER_TOPK)
    v2, _, rank2 = _extract_top(s2, PEER_TOPK, v2_ref)
    v2_all = v2_ref[...]
    v2_top8 = v2_all[:8]
    row8 = lax.broadcasted_iota(jnp.int32, v2_top8.shape, 0)
    cands = [v1[0] + v2_all]
    for a in range(1, PEER_TOPK):
        nb = PEER_TOPK // (a + 1)
        cands.append(jnp.where(row8 < nb, v1[a] + v2_top8, -jnp.inf))
    cand = jnp.concatenate(cands, axis=0)
    best, _, _ = _extract_top(cand, PEER_TOPK)
    cmax, tau = best[0], best[-1]
    z = jnp.sum(jnp.where(cand >= tau, jnp.exp(cand - cmax), 0.0), axis=0, keepdims=True)
    iota1 = lax.broadcasted_iota(jnp.int32, s1.shape, 0)
    n1 = jnp.zeros(s1.shape, F32)
    for a in range(PEER_TOPK):
        n_a = jnp.sum(jnp.where(v1[a] + v2_all >= tau, 1.0, 0.0), axis=0, keepdims=True)
        n1 = jnp.where(iota1 == idx1[a], n_a, n1)
    n1_ref[0] = _bf16_pair(n1)
    e1_ref[0] = _bf16_pair(jnp.exp(s1 - v1[0]) / z)
    r2_ref[0] = _bf16_pack_rows(rank2)
    e2_ref[0] = _bf16_pack_rows(jnp.exp(s2 - v2[0]))


def _peer_topk(q, sub_k):
    n = q.shape[0]
    tt = 256
    assert n % tt == 0
    row = pl.BlockSpec((1, PEER_KEYS, tt), lambda i, h: (h, 0, i))
    tile = pl.BlockSpec((1, PEER_KEYS // 2, tt), lambda i, h: (h, 0, i))
    row_shape = jax.ShapeDtypeStruct((PEER_HEADS, PEER_KEYS, n), jnp.uint32)
    tile_shape = jax.ShapeDtypeStruct((PEER_HEADS, PEER_KEYS // 2, n), jnp.uint32)
    return pl.pallas_call(
        _peer_topk_kernel,
        grid=(n // tt, PEER_HEADS),
        in_specs=[pl.BlockSpec((tt, PEER_DKEY), lambda i, h: (i, h)),
                  pl.BlockSpec((PEER_KEYS, PEER_DKEY // 2), lambda i, h: (0, 0)),
                  pl.BlockSpec((PEER_KEYS, PEER_DKEY // 2), lambda i, h: (0, 0))],
        out_specs=[row, row, tile, tile],
        out_shape=[row_shape, row_shape, tile_shape, tile_shape],
        scratch_shapes=[pltpu.VMEM((PEER_TOPK, tt), F32)],
        compiler_params=_params("parallel", "parallel"),
        name="peer_topk",
    )(q, sub_k[0], sub_k[1])


PEER_HALF_I1 = 4
PEER_HALF = PEER_HALF_I1 * PEER_KEYS


def _peer_dense_kernel(xn_ref, u_ref, vta_ref, vtb_ref, n1_ref, e1_ref, r2_ref, e2_ref, h_ref, gout_ref,
                       o_ref, a_ref, wa_ref, wb_ref, acc_ref, *, tt, final_norm):
    e = pl.program_id(1)
    last = pl.num_programs(1) - 1

    @pl.when(e == 0)
    def _():
        acc_ref[...] = jnp.zeros_like(acc_ref)
        wb_ref[...] = jnp.zeros_like(wb_ref)

    live = jnp.where(e < last, jnp.uint32(1), jnp.uint32(0))
    n_tiles = PEER_KEYS // BF16_ROWS

    def packed_row(ref, hd, r, lanes, scale=None):
        word = ref[hd, r:r + 1, lanes]
        if scale is not None:
            word = word * scale
        return pltpu.bitcast(jnp.broadcast_to(word, (8, 128)), BF16)[None]

    def as_bf16_tiles(words):
        return pltpu.bitcast(words.reshape(n_tiles, 8, 128), BF16)

    def build(half, w_ref):
        rows_u = slice(half * PEER_HALF, (half + 1) * PEER_HALF)
        a_ref[...] = lax.dot_general(u_ref[rows_u, :], xn_ref[...], _NT, preferred_element_type=F32)
        for lg in range(tt // 128):
            lanes = slice(lg * 128, (lg + 1) * 128)
            for il in range(PEER_HALF_I1):
                r = half * PEER_HALF_I1 + il
                rows = slice(il * PEER_KEYS, (il + 1) * PEER_KEYS)
                gate = jnp.zeros((n_tiles, BF16_ROWS, 128), BF16)
                for hd in range(PEER_HEADS):
                    n1b = packed_row(n1_ref, hd, r, lanes, live)
                    e1b = packed_row(e1_ref, hd, r, lanes)
                    r2 = as_bf16_tiles(r2_ref[hd, :, lanes])
                    e2 = as_bf16_tiles(e2_ref[hd, :, lanes])
                    gate = gate + jnp.where(r2 < n1b, e2 * e1b, jnp.zeros_like(e2))
                gate = _bf16_unpack_rows(pltpu.bitcast(gate, jnp.uint32))
                act = _gelu_tanh(a_ref[rows, lanes]).reshape(n_tiles, 2, 8, 128)
                w_ref[rows, lanes] = (act * gate).reshape(PEER_KEYS, 128).astype(BF16)

    acc_ref[...] += jnp.dot(vtb_ref[...], wb_ref[...], preferred_element_type=F32)
    build(0, wa_ref)
    acc_ref[...] += jnp.dot(vta_ref[...], wa_ref[...], preferred_element_type=F32)
    build(1, wb_ref)

    @pl.when(e == last)
    def _():
        out = h_ref[...] + acc_ref[...].T
        if final_norm:
            out = _rms(out, gout_ref[...])
        o_ref[...] = out


def _peer_dense(xn_bf16, u_bf16, vt_bf16, n1, e1, r2, e2, h, g_out, final_norm):
    n = h.shape[0]
    tt = min(512, n)
    n_exp = u_bf16.shape[0]
    n_e = n_exp // (2 * PEER_HALF)
    tok = lambda i, e: (i, 0)
    cur = lambda e: jnp.minimum(e, n_e - 1)
    return pl.pallas_call(
        functools.partial(_peer_dense_kernel, tt=tt, final_norm=final_norm),
        grid=(n // tt, n_e + 1),
        in_specs=[pl.BlockSpec((tt, D_MODEL), tok),
                  pl.BlockSpec((2 * PEER_HALF, D_MODEL), lambda i, e: (cur(e), 0)),
                  pl.BlockSpec((D_MODEL, PEER_HALF), lambda i, e: (0, 2 * cur(e))),
                  pl.BlockSpec((D_MODEL, PEER_HALF), lambda i, e: (0, jnp.maximum(2 * e - 1, 0))),
                  pl.BlockSpec((PEER_HEADS, 2 * PEER_HALF_I1, tt), lambda i, e: (0, cur(e), i)),
                  pl.BlockSpec((PEER_HEADS, 2 * PEER_HALF_I1, tt), lambda i, e: (0, cur(e), i)),
                  pl.BlockSpec((PEER_HEADS, PEER_KEYS // 2, tt), lambda i, e: (0, 0, i)),
                  pl.BlockSpec((PEER_HEADS, PEER_KEYS // 2, tt), lambda i, e: (0, 0, i)),
                  pl.BlockSpec((tt, D_MODEL), tok),
                  pl.BlockSpec((1, D_MODEL), lambda i, e: (0, 0))],
        out_specs=pl.BlockSpec((tt, D_MODEL), tok),
        out_shape=jax.ShapeDtypeStruct((n, D_MODEL), F32),
        scratch_shapes=[pltpu.VMEM((PEER_HALF, tt), F32), pltpu.VMEM((PEER_HALF, tt), BF16),
                        pltpu.VMEM((PEER_HALF, tt), BF16), pltpu.VMEM((D_MODEL, tt), F32)],
        compiler_params=_params("parallel", "arbitrary"),
        name="peer_dense",
    )(xn_bf16, u_bf16, vt_bf16, vt_bf16, n1, e1, r2, e2, h, g_out.reshape(1, D_MODEL))


def _transpose_kernel(v_ref, o_ref):
    o_ref[...] = v_ref[...].T.astype(o_ref.dtype)


def _transpose_bf16(v):
    r, c = v.shape
    tr = 512
    return pl.pallas_call(
        _transpose_kernel,
        grid=(r // tr,),
        in_specs=[pl.BlockSpec((tr, c), lambda i: (i, 0))],
        out_specs=pl.BlockSpec((c, tr), lambda i: (0, i)),
        out_shape=jax.ShapeDtypeStruct((c, r), BF16),
        compiler_params=_params("parallel"),
        name="transpose_bf16",
    )(v)


def _peer_layer(h, g_norm, w_q, sub_k, u_emb, v_emb, g_out, final_norm):
    q, xn = _norm_matmul(h, g_norm, w_q.astype(BF16), "peer")
    n1, e1, r2, e2 = _peer_topk(q, sub_k)
    return _peer_dense(xn, u_emb.astype(BF16), _transpose_bf16(v_emb), n1, e1, r2, e2, h, g_out, final_norm)


GATHER_PAGES = 8
GATHER_ROWS = GATHER_PAGES * PAGE_SIZE


def _gather_kernel(pt_ref, *refs, n_steps):
    page_refs, new_ref, o_ref = refs[:GATHER_PAGES], refs[GATHER_PAGES], refs[GATHER_PAGES + 1]
    p = pl.program_id(1)

    @pl.when(p < n_steps)
    def _():
        for k, page_ref in enumerate(page_refs):
            o_ref[0, k * PAGE_SIZE:(k + 1) * PAGE_SIZE, :] = page_ref[0].astype(o_ref.dtype)

    @pl.when(p >= n_steps)
    def _():
        o_ref[0] = new_ref[0].astype(o_ref.dtype)


def _gather_pages(cache, page_table, new_rows, out_dtype):
    b, n_pages = page_table.shape
    assert n_pages % GATHER_PAGES == 0 and new_rows.shape[1] <= GATHER_ROWS
    n_steps = n_pages // GATHER_PAGES
    new_pad = jnp.pad(new_rows, ((0, 0), (0, GATHER_ROWS - new_rows.shape[1]), (0, 0)))

    def page_spec(k):
        return pl.BlockSpec((1, PAGE_SIZE, KV_COLS),
                            lambda i, p, pt: (pt[i, jnp.minimum(p, n_steps - 1) * GATHER_PAGES + k], 0, 0))

    grid_spec = pltpu.PrefetchScalarGridSpec(
        num_scalar_prefetch=1,
        grid=(b, n_steps + 1),
        in_specs=[page_spec(k) for k in range(GATHER_PAGES)]
        + [pl.BlockSpec((1, GATHER_ROWS, KV_COLS), lambda i, p, pt: (i, 0, 0))],
        out_specs=pl.BlockSpec((1, GATHER_ROWS, KV_COLS), lambda i, p, pt: (i, p, 0)),
    )
    return pl.pallas_call(
        functools.partial(_gather_kernel, n_steps=n_steps),
        grid_spec=grid_spec,
        out_shape=jax.ShapeDtypeStruct((b, (n_steps + 1) * GATHER_ROWS, KV_COLS), out_dtype),
        compiler_params=_params("parallel", "arbitrary"),
        name="gather_pages",
    )(page_table, *([cache] * GATHER_PAGES), new_pad)


CMP_TILE = 128


def _compress_kernel(main_ref, halo_ref, pe_ref, w1k_ref, w1v_ref, w2k_ref, w2v_ref, o_ref, win_ref):
    n_tok = CMP_TILE * CMP_STRIDE
    n_lane_tiles = KV_COLS // 128
    for c in range(n_lane_tiles):
        win_ref[c, 0:n_tok, :] = main_ref[0, :, c * 128:(c + 1) * 128]
        win_ref[c, n_tok:n_tok + CMP_STRIDE, :] = halo_ref[0, :, c * 128:(c + 1) * 128]
    hk = jnp.zeros((CMP_TILE, N_KV_HEADS * CMP_HIDDEN), F32)
    hv = jnp.zeros((CMP_TILE, N_KV_HEADS * CMP_HIDDEN), F32)
    for s in range(CMP_BLOCK):
        x = jnp.concatenate([win_ref[c, pl.ds(s, CMP_TILE, stride=CMP_STRIDE), :] for c in range(n_lane_tiles)],
                            axis=1)
        x = (x + pe_ref[s]).astype(BF16)
        hk = hk + jnp.dot(x[:, :K_COLS], w1k_ref[s], preferred_element_type=F32)
        hv = hv + jnp.dot(x[:, K_COLS:], w1v_ref[s], preferred_element_type=F32)
    o_ref[0, :, :K_COLS] = jnp.dot(_gelu_tanh(hk).astype(BF16), w2k_ref[...], preferred_element_type=F32)
    o_ref[0, :, K_COLS:] = jnp.dot(_gelu_tanh(hv).astype(BF16), w2v_ref[...], preferred_element_type=F32)


def _block_diag(w):
    eye = jnp.eye(N_KV_HEADS, dtype=w.dtype)
    out = jnp.einsum("gh,...ab->...gahb", eye, w)
    return out.reshape(w.shape[:-2] + (N_KV_HEADS * w.shape[-2], N_KV_HEADS * w.shape[-1]))


def _compress(rows, n_tiles, cmp_pe, cmp_w1, cmp_w2):
    b, length = rows.shape[:2]
    n_tok = CMP_TILE * CMP_STRIDE
    last_halo = length // CMP_STRIDE - 1
    w1 = cmp_w1.reshape(2, CMP_BLOCK, HEAD_DIM, CMP_HIDDEN)
    w1k, w1v = _block_diag(w1[0]).astype(BF16), _block_diag(w1[1]).astype(BF16)
    w2k, w2v = _block_diag(cmp_w2[0]).astype(BF16), _block_diag(cmp_w2[1]).astype(BF16)
    pe = jnp.concatenate([jnp.tile(cmp_pe[0], (1, N_KV_HEADS)), jnp.tile(cmp_pe[1], (1, N_KV_HEADS))], axis=1)
    pe = pe.reshape(CMP_BLOCK, 1, KV_COLS)
    const3 = lambda i, j: (0, 0, 0)
    return pl.pallas_call(
        _compress_kernel,
        grid=(b, n_tiles),
        in_specs=[pl.BlockSpec((1, n_tok, KV_COLS), lambda i, j: (i, j, 0)),
                  pl.BlockSpec((1, CMP_STRIDE, KV_COLS),
                               lambda i, j: (i, jnp.minimum((j + 1) * CMP_TILE, last_halo), 0)),
                  pl.BlockSpec((CMP_BLOCK, 1, KV_COLS), const3),
                  pl.BlockSpec((CMP_BLOCK, K_COLS, N_KV_HEADS * CMP_HIDDEN), const3),
                  pl.BlockSpec((CMP_BLOCK, K_COLS, N_KV_HEADS * CMP_HIDDEN), const3),
                  pl.BlockSpec((N_KV_HEADS * CMP_HIDDEN, K_COLS), lambda i, j: (0, 0)),
                  pl.BlockSpec((N_KV_HEADS * CMP_HIDDEN, K_COLS), lambda i, j: (0, 0))],
        out_specs=pl.BlockSpec((1, CMP_TILE, KV_COLS), lambda i, j: (i, j, 0)),
        out_shape=jax.ShapeDtypeStruct((b, n_tiles * CMP_TILE, KV_COLS), F32),
        scratch_shapes=[pltpu.VMEM((KV_COLS // 128, n_tok + CMP_STRIDE, 128), F32)],
        compiler_params=_params("parallel", "parallel"),
        name="cmp_compress",
    )(rows, rows, pe, w1k, w1v, w2k, w2v)


def _head_perm():
    c = np.arange(D_MODEL)
    h, g, d = c // K_COLS, (c % K_COLS) // HEAD_DIM, c % HEAD_DIM
    return (g * HEADS_PER_KV + h) * HEAD_DIM + d


def _gate_expand(branch):
    c = np.arange(D_MODEL)
    h, g = c // K_COLS, (c % K_COLS) // HEAD_DIM
    m = np.zeros((GATE_COLS, D_MODEL), np.float32)
    m[(g * HEADS_PER_KV + h) * N_BRANCH + branch, c] = 1.0
    return jnp.asarray(m)


def _block_diag_queries(q):
    tq = q.shape[0]
    lane_g = lax.broadcasted_iota(jnp.int32, (tq, K_COLS), 1) // HEAD_DIM
    blocks = []
    for g in range(N_KV_HEADS):
        for h in range(HEADS_PER_KV):
            blocks.append(jnp.where(lane_g == g, q[:, h * K_COLS:(h + 1) * K_COLS], 0.0))
    return jnp.concatenate(blocks, axis=0)


def _diag_heads(acc, tq):
    lane_g = lax.broadcasted_iota(jnp.int32, (tq, K_COLS), 1) // HEAD_DIM
    outs = []
    for h in range(HEADS_PER_KV):
        o = jnp.zeros((tq, K_COLS), F32)
        for g in range(N_KV_HEADS):
            r0 = (g * HEADS_PER_KV + h) * tq
            o = o + jnp.where(lane_g == g, acc[r0:r0 + tq], 0.0)
        outs.append(o)
    return jnp.concatenate(outs, axis=1)


_NT = (((1,), (1,)), ((), ()))


def _cmp_attn_kernel(q_ref, gate_ref, kvc_ref, ovt_ref, eg_ref, oc_ref, sel_ref, sc_ref,
                     *, tq, pos0, n_slc_loop):
    q0 = pos0 + pl.program_id(1) * tq
    n_pad = kvc_ref.shape[1]
    s_pad = ovt_ref.shape[0]
    rows = N_HEADS * tq
    qbd = _block_diag_queries(q_ref[0]).astype(BF16)
    kc = kvc_ref[0, :, :K_COLS].astype(BF16)
    vc = kvc_ref[0, :, K_COLS:].astype(BF16)
    s = lax.dot_general(qbd, kc, _NT, preferred_element_type=F32)
    t_row = q0 + (lax.broadcasted_iota(jnp.int32, (rows, n_pad), 0) & (tq - 1))
    n_col = lax.broadcasted_iota(jnp.int32, (rows, n_pad), 1)
    vis = (n_col * CMP_STRIDE + (CMP_BLOCK - 1)) <= t_row
    m = jnp.max(jnp.where(vis, s, -jnp.inf), axis=-1, keepdims=True)
    m = jnp.where(m > -jnp.inf, m, 0.0)
    e = jnp.where(vis, jnp.exp(s - m), 0.0)
    p = e / jnp.maximum(jnp.sum(e, axis=-1, keepdims=True), 1e-30)
    oc = _diag_heads(jnp.dot(p.astype(BF16), vc, preferred_element_type=F32), tq)
    oc_ref[0] = oc * jnp.dot(gate_ref[0], eg_ref[...], preferred_element_type=F32)

    psum = jnp.concatenate(
        [sum(p[(g * HEADS_PER_KV + h) * tq:(g * HEADS_PER_KV + h + 1) * tq] for h in range(HEADS_PER_KV))
         for g in range(N_KV_HEADS)], axis=0)
    imp_t = lax.dot_general(ovt_ref[...], psum, _NT, preferred_element_type=F32)
    cols = N_KV_HEADS * tq
    blk = lax.broadcasted_iota(jnp.int32, (s_pad, cols), 0)
    cur = (q0 + (lax.broadcasted_iota(jnp.int32, (s_pad, cols), 1) & (tq - 1))) // SLC_BLOCK
    forced = (blk == 0) | (blk == cur) | (blk == cur - 1)
    score = jnp.where(forced, jnp.inf, jnp.where(blk <= cur, imp_t, -jnp.inf))
    sc_ref[...] = score

    def rank_body(jb, rank):
        rows8 = sc_ref[pl.ds(pl.multiple_of(jb * 8, 8), 8), :]
        for r in range(8):
            row = rows8[r:r + 1]
            tie = jnp.where(row == score, jnp.where(jb * 8 + r < blk, 1.0, 0.0), 0.0)
            rank = rank + jnp.where(row > score, 1.0, tie)
        return rank

    rank = lax.fori_loop(0, n_slc_loop // 8, rank_body, jnp.zeros((s_pad, cols), F32))
    sel_t = jnp.where(rank < float(N_SELECT), jnp.where(blk <= cur, 1.0, 0.0), 0.0).astype(BF16)
    ci = lax.broadcasted_iota(jnp.int32, (cols, cols), 0)
    cj = lax.broadcasted_iota(jnp.int32, (cols, cols), 1)
    eye = jnp.where(ci == cj, 1.0, 0.0).astype(BF16)
    sel_ref[0, 0] = lax.dot_general(eye, sel_t, _NT, preferred_element_type=F32)


def _cmp_attn(q, gates, kvc, ov_t, pos0, tq, n_slc):
    b, t, _ = q.shape
    s_pad = ov_t.shape[0]
    n_pad = kvc.shape[1]
    n_slc_loop = -(-n_slc // 8) * 8
    return pl.pallas_call(
        functools.partial(_cmp_attn_kernel, tq=tq, pos0=pos0, n_slc_loop=n_slc_loop),
        grid=(b, t // tq),
        in_specs=[pl.BlockSpec((1, tq, D_MODEL), lambda i, j: (i, j, 0)),
                  pl.BlockSpec((1, tq, GATE_COLS), lambda i, j: (i, j, 0)),
                  pl.BlockSpec((1, n_pad, KV_COLS), lambda i, j: (i, 0, 0)),
                  pl.BlockSpec((s_pad, n_pad), lambda i, j: (0, 0)),
                  pl.BlockSpec((GATE_COLS, D_MODEL), lambda i, j: (0, 0))],
        out_specs=[pl.BlockSpec((1, tq, D_MODEL), lambda i, j: (i, j, 0)),
                   pl.BlockSpec((1, 1, N_KV_HEADS * tq, s_pad), lambda i, j: (i, j, 0, 0))],
        out_shape=[jax.ShapeDtypeStruct((b, t, D_MODEL), F32),
                   jax.ShapeDtypeStruct((b, t // tq, N_KV_HEADS * tq, s_pad), F32)],
        scratch_shapes=[pltpu.VMEM((s_pad, N_KV_HEADS * tq), F32)],
        compiler_params=_params("parallel", "parallel"),
        name="nsa_cmp_select",
    )(q, gates, kvc, ov_t, _gate_expand(0))


WIN_KEYS = 768
ATTN_ROWS = 64


def _lane_tiles(x):
    return [x[:, i * 128:(i + 1) * 128] for i in range(x.shape[1] // 128)]


def _row_max(x):
    return jnp.max(functools.reduce(jnp.maximum, _lane_tiles(x)), axis=-1, keepdims=True)


def _row_sum(x):
    return jnp.sum(functools.reduce(jnp.add, _lane_tiles(x)), axis=-1, keepdims=True)


def _sparse_attn_kernel(q_ref, gate_ref, sel_ref, oc_ref, kslc_ref, kwin_ref, egs_ref, egw_ref, o_ref,
                        qaug_ref, s_ref, p_ref, sw_ref, pw_ref, m_ref, l_ref, alpha_ref, acc_ref,
                        *, tq, tk, pos0, win_pos0):
    q0 = pos0 + pl.program_id(1) * tq
    rows = N_HEADS * tq
    s_pad = sel_ref.shape[3]
    rb = min(ATTN_ROWS, rows)

    qaug_ref[:, 0:K_COLS] = _block_diag_queries(q_ref[0]).astype(BF16)
    not_sel = sel_ref[0, 0] - 1.0
    qaug_ref[:, K_COLS:] = jnp.concatenate(
        [not_sel[g * tq:(g + 1) * tq] for g in range(N_KV_HEADS) for _ in range(HEADS_PER_KV)],
        axis=0).astype(BF16)

    def q_pos(r0, width):
        return q0 + ((r0 + lax.broadcasted_iota(jnp.int32, (rb, width), 0)) & (tq - 1))

    m_ref[...] = jnp.full_like(m_ref, NEG_BIG)
    l_ref[...] = jnp.zeros_like(l_ref)
    acc_ref[...] = jnp.zeros_like(acc_ref)
    blk_lane = lax.broadcasted_iota(jnp.int32, (tk, s_pad), 1)
    blk_key = lax.broadcasted_iota(jnp.int32, (tk, s_pad), 0) // SLC_BLOCK
    k_col = lax.broadcasted_iota(jnp.int32, (rb, tk), 1)

    half_rows = rows // 2

    def slc_tile(j, causal):
        ks = pl.ds(pl.multiple_of(j * tk, tk), tk)
        onehot = jnp.where(blk_lane == j * (tk // SLC_BLOCK) + blk_key, -NEG_BIG, 0.0).astype(BF16)
        kaug = jnp.concatenate([kslc_ref[0, ks, 0:K_COLS], onehot], axis=1)
        vt = kslc_ref[0, ks, K_COLS:KV_COLS]
        for hf in range(2):
            s_ref[hf] = lax.dot_general(qaug_ref[hf * half_rows:(hf + 1) * half_rows, :], kaug, _NT,
                                        preferred_element_type=F32)
        for hf in range(2):
            for r0 in range(0, half_rows, rb):
                blk = slice(hf * half_rows + r0, hf * half_rows + r0 + rb)
                s = s_ref[hf, r0:r0 + rb, :]
                if causal:
                    s = jnp.where(j * tk + k_col <= q_pos(hf * half_rows + r0, tk), s, NEG_BIG)
                m_old = m_ref[blk, :]
                m_new = jnp.maximum(m_old, _row_max(s))
                p = jnp.exp(s - m_new)
                alpha = jnp.exp(m_old - m_new)
                l_ref[blk, :] = alpha * l_ref[blk, :] + _row_sum(p)
                m_ref[blk, :] = m_new
                alpha_ref[blk, :] = alpha
                p_ref[hf, r0:r0 + rb, :] = p.astype(BF16)
            hrows = slice(hf * half_rows, (hf + 1) * half_rows)
            acc_ref[hrows, :] = alpha_ref[hrows, :] * acc_ref[hrows, :] + jnp.dot(
                p_ref[hf], vt, preferred_element_type=F32)

    last_tile = (q0 + tq - 1) // tk
    lax.fori_loop(0, last_tile, lambda j, c: (slc_tile(j, False), c)[1], 0)
    slc_tile(last_tile, True)
    o_s = _diag_heads(acc_ref[...] / jnp.maximum(l_ref[...], 1e-30), tq)

    lw = kwin_ref.shape[1]
    w0 = jnp.clip((q0 - win_pos0 - WINDOW) // 128 * 128, 0, lw - WIN_KEYS)
    wk = pl.ds(pl.multiple_of(w0, 128), WIN_KEYS)
    sw_ref[...] = lax.dot_general(qaug_ref[:, 0:K_COLS], kwin_ref[0, wk, 0:K_COLS], _NT,
                                  preferred_element_type=F32)
    kpos = win_pos0 + w0 + lax.broadcasted_iota(jnp.int32, (rb, WIN_KEYS), 1)
    for r0 in range(0, rows, rb):
        blk = slice(r0, r0 + rb)
        t = q_pos(r0, WIN_KEYS)
        s = jnp.where((kpos <= t) & (kpos > t - WINDOW), sw_ref[blk, :], NEG_BIG)
        p = jnp.exp(s - _row_max(s))
        l_ref[blk, :] = _row_sum(p)
        pw_ref[blk, :] = p.astype(BF16)
    o_w = jnp.dot(pw_ref[...], kwin_ref[0, wk, K_COLS:KV_COLS], preferred_element_type=F32)
    o_w = o_w / jnp.maximum(l_ref[...], 1e-30)
    o_w = _diag_heads(o_w, tq)

    gate = gate_ref[0]
    g_s = jnp.dot(gate, egs_ref[...], preferred_element_type=F32)
    g_w = jnp.dot(gate, egw_ref[...], preferred_element_type=F32)
    o_ref[0] = (oc_ref[0] + g_s * o_s + g_w * o_w).astype(o_ref.dtype)


def _sparse_attn(q, gates, sel, oc, kslc, kwin, slc_col, win_col, pos0, win_pos0, tq, tk):
    b, t, _ = q.shape
    s_pad = sel.shape[3]
    ls, lw = kslc.shape[1], kwin.shape[1]
    assert ls % tk == 0 and lw % 128 == 0 and lw >= WIN_KEYS and tk % tq == 0
    rows = N_HEADS * tq
    qtile = lambda i, j: (i, j, 0)
    return pl.pallas_call(
        functools.partial(_sparse_attn_kernel, tq=tq, tk=tk, pos0=pos0, win_pos0=win_pos0),
        grid=(b, t // tq),
        in_specs=[pl.BlockSpec((1, tq, D_MODEL), qtile),
                  pl.BlockSpec((1, tq, GATE_COLS), qtile),
                  pl.BlockSpec((1, 1, N_KV_HEADS * tq, s_pad), lambda i, j: (i, j, 0, 0)),
                  pl.BlockSpec((1, tq, D_MODEL), qtile),
                  pl.BlockSpec((1, ls, KV_COLS), lambda i, j: (i, 0, slc_col)),
                  pl.BlockSpec((1, lw, KV_COLS), lambda i, j: (i, 0, win_col)),
                  pl.BlockSpec((GATE_COLS, D_MODEL), lambda i, j: (0, 0)),
                  pl.BlockSpec((GATE_COLS, D_MODEL), lambda i, j: (0, 0))],
        out_specs=pl.BlockSpec((1, tq, D_MODEL), qtile),
        out_shape=jax.ShapeDtypeStruct((b, t, D_MODEL), F32),
        scratch_shapes=[pltpu.VMEM((rows, K_COLS + s_pad), BF16),
                        pltpu.VMEM((2, rows // 2, tk), F32), pltpu.VMEM((2, rows // 2, tk), BF16),
                        pltpu.VMEM((rows, WIN_KEYS), F32), pltpu.VMEM((rows, WIN_KEYS), BF16),
                        pltpu.VMEM((rows, 1), F32), pltpu.VMEM((rows, 1), F32), pltpu.VMEM((rows, 1), F32),
                        pltpu.VMEM((rows, K_COLS), F32)],
        compiler_params=_params("parallel", "arbitrary"),
        name="nsa_select_window",
    )(q, gates, sel, oc, kslc, kwin, _gate_expand(1), _gate_expand(2))


ATTN_TK = 512


def _overlap_t(s_pad, n_pad, n_slc, n_cmp):
    n = np.arange(n_pad)[None, :]
    s = np.arange(s_pad)[:, None]
    ov = ((n * CMP_STRIDE < s * SLC_BLOCK + SLC_BLOCK) & (n * CMP_STRIDE + CMP_BLOCK - 1 >= s * SLC_BLOCK)
          & (n < n_cmp) & (s < n_slc))
    return jnp.asarray(ov.astype(np.float32))


def _trunk(x, conv_past, past, pos0, p):
    b, t, _ = x.shape
    n = b * t
    kvh = (2, N_KV_HEADS, HEAD_DIM)
    h = x.reshape(n, D_MODEL)

    (u,) = _norm_matmul(h, p["norm_mix_g"][0], p["conv_w_in"][0].astype(BF16), "glu")
    u = u.reshape(b, t, D_MODEL)
    h = _conv_module(u, conv_past, h.reshape(b, t, D_MODEL), p["conv_dw_k"][0], p["conv_dw_b"][0],
                     p["conv_ln_g"][0], p["conv_ln_b"][0], p["conv_w_out"][0].astype(BF16)).reshape(n, D_MODEL)
    conv_state = jnp.concatenate([conv_past, u], axis=1)[:, -(CONV_WIDTH - 1):][None]
    h = _peer_layer(h, p["norm_ffn_g"][0], p["peer_w_q"][0], p["peer_sub_k"][0], p["peer_u"][0],
                    p["peer_v"][0], p["norm_out_g"], False)

    kv, kv_bf16 = _norm_matmul(h, p["norm_kv_g"], p["nsa_w_kv"].astype(BF16), "kv")
    kv = kv.reshape(b, t, N_BRANCH * KV_COLS)
    kv_bf16 = kv_bf16.reshape(b, t, N_BRANCH * KV_COLS)
    new_cmp, new_slc, new_win = (kv[..., i * KV_COLS:(i + 1) * KV_COLS] for i in range(N_BRANCH))
    if past is None:
        length, win_pos0 = t, pos0
        cmp_rows, k_slc, k_win, slc_col, win_col = kv, kv_bf16, kv_bf16, 1, 2
        win_all = new_win
    else:
        cache_cmp, cache_slc, cache_win, page_table = past
        length, win_pos0 = pos0 + t, pos0 - cache_win.shape[1]
        cmp_rows = _gather_pages(cache_cmp, page_table, new_cmp, F32)
        k_slc = _gather_pages(cache_slc, page_table, new_slc, BF16)
        win_all = jnp.concatenate([cache_win, new_win], axis=1)
        lw = max(WIN_KEYS, -(-win_all.shape[1] // 128) * 128)
        k_win = jnp.pad(win_all, ((0, 0), (0, lw - win_all.shape[1]), (0, 0))).astype(BF16)
        slc_col, win_col = 0, 0
    n_cmp = (length - CMP_BLOCK) // CMP_STRIDE + 1
    n_slc = -(-length // SLC_BLOCK)
    n_tiles = -(-n_cmp // CMP_TILE)
    kvc = _compress(cmp_rows, n_tiles, p["nsa_cmp_pe"], p["nsa_cmp_w1"], p["nsa_cmp_w2"])

    perm = _head_perm()
    w_in = p["nsa_w_in"][0]
    w_in = jnp.concatenate([w_in[:, perm], w_in[:, D_MODEL:],
                            jnp.zeros((D_MODEL, GATE_COLS - N_HEADS * N_BRANCH), F32)], axis=1).astype(BF16)
    q, gates = _norm_matmul(h, p["norm_mix_g"][1], w_in, "nsa")
    q = q.reshape(b, t, D_MODEL)
    gates = gates.reshape(b, t, GATE_COLS)
    tq = min(64, t)
    s_pad = -(-n_slc // 128) * 128
    oc, sel = _cmp_attn(q, gates, kvc, _overlap_t(s_pad, kvc.shape[1], n_slc, n_cmp), pos0, tq, n_slc)
    o = _sparse_attn(q, gates, sel, oc, k_slc, k_win, slc_col, win_col, pos0, win_pos0, tq, ATTN_TK)
    h = _matmul_res(o.reshape(n, D_MODEL), p["nsa_w_o"][0][perm, :].astype(BF16), h)
    y = _peer_layer(h, p["norm_ffn_g"][1], p["peer_w_q"][1], p["peer_sub_k"][1], p["peer_u"][1],
                    p["peer_v"][1], p["norm_out_g"], True)
    keep = min(WINDOW, win_all.shape[1])
    shape5 = lambda a: a.reshape(a.shape[:2] + kvh)
    return (y.reshape(b, t, D_MODEL), conv_state, shape5(new_cmp), shape5(new_slc), shape5(win_all[:, -keep:]))


def kernel(x_prompt, x_sample, state_conv, cache_cmp_kv, cache_slc_kv, cache_win_kv, page_table, norm_mix_g, norm_ffn_g, norm_kv_g, norm_out_g, conv_w_in, conv_dw_k, conv_dw_b, conv_ln_g, conv_ln_b, conv_w_out, nsa_w_kv, nsa_cmp_pe, nsa_cmp_w1, nsa_cmp_w2, nsa_w_in, nsa_w_o, peer_w_q, peer_sub_k, peer_u, peer_v):
    p = dict(norm_mix_g=norm_mix_g, norm_ffn_g=norm_ffn_g, norm_kv_g=norm_kv_g, norm_out_g=norm_out_g,
             conv_w_in=conv_w_in, conv_dw_k=conv_dw_k, conv_dw_b=conv_dw_b, conv_ln_g=conv_ln_g,
             conv_ln_b=conv_ln_b, conv_w_out=conv_w_out, nsa_w_kv=nsa_w_kv, nsa_cmp_pe=nsa_cmp_pe,
             nsa_cmp_w1=nsa_cmp_w1, nsa_cmp_w2=nsa_cmp_w2, nsa_w_in=nsa_w_in, nsa_w_o=nsa_w_o,
             peer_w_q=peer_w_q, peer_sub_k=peer_sub_k, peer_u=peer_u, peer_v=peer_v)
    bsz = x_prompt.shape[0]
    conv0 = jnp.zeros((bsz, CONV_WIDTH - 1, D_MODEL), x_prompt.dtype)
    y_p, conv_p, cmp_p, slc_p, win_p = _trunk(x_prompt, conv0, None, 0, p)
    dec_b, n_pages = page_table.shape
    flat = lambda c: c.reshape(c.shape[0], c.shape[1], KV_COLS)
    past = (flat(cache_cmp_kv), flat(cache_slc_kv), flat(cache_win_kv), page_table)
    y_s, conv_s, cmp_s, slc_s, win_s = _trunk(x_sample, state_conv[0], past, n_pages * PAGE_SIZE, p)
    return (y_p, y_s, conv_p, cmp_p, slc_p, win_p, conv_s, cmp_s, slc_s, win_s)
```

```python
import functools
import math

import numpy as np
import jax
import jax.numpy as jnp
from jax import lax
from jax.experimental import pallas as pl
from jax.experimental.pallas import tpu as pltpu

F32 = jnp.float32
BF16 = jnp.bfloat16

D_MODEL = 1024
EPS = 1e-6
CONV_WIDTH = 31
CONV_HALO = 32
N_HEADS = 16
HEAD_DIM = 64
N_KV_HEADS = 4
HEADS_PER_KV = 4
N_BRANCH = 3
KV_COLS = 2 * N_KV_HEADS * HEAD_DIM
K_COLS = N_KV_HEADS * HEAD_DIM
CMP_BLOCK = 32
CMP_STRIDE = 16
CMP_HIDDEN = 128
SLC_BLOCK = 64
N_SELECT = 16
WINDOW = 512
PAGE_SIZE = 128
PEER_HEADS = 8
PEER_KEYS = 128
PEER_TOPK = 16
PEER_DKEY = 256
GATE_COLS = 128

VMEM_LIMIT_BYTES = 56 * 1024 * 1024
NEG_BIG = -1e30


def _params(*sem):
    return pltpu.CompilerParams(dimension_semantics=sem, vmem_limit_bytes=VMEM_LIMIT_BYTES)


def _gelu_tanh(x):
    return 0.5 * x * (1.0 + jnp.tanh(0.7978845608028654 * (x + 0.044715 * (x * x * x))))


def _sigmoid(x):
    return 1.0 / (1.0 + jnp.exp(-x))


def _rms(x, g):
    return x * lax.rsqrt(jnp.mean(x * x, axis=-1, keepdims=True) + EPS) * g


def _norm_matmul_kernel(x_ref, g_ref, w_ref, *out_refs, mode):
    xn = _rms(x_ref[...], g_ref[...])
    xb = xn.astype(BF16)
    y = jnp.dot(xb, w_ref[...], preferred_element_type=F32)
    if mode == "glu":
        out_refs[0][...] = y[:, :D_MODEL] * _sigmoid(y[:, D_MODEL:])
    elif mode == "peer":
        out_refs[0][...] = y
        out_refs[1][...] = xb
    elif mode == "kv":
        out_refs[0][...] = y
        out_refs[1][...] = y.astype(BF16)
    elif mode == "nsa":
        out_refs[0][...] = y[:, :D_MODEL] * (HEAD_DIM ** -0.5)
        out_refs[1][...] = _sigmoid(y[:, D_MODEL:])
    else:
        raise ValueError(mode)


def _norm_matmul(x, g, w_bf16, mode):
    n = x.shape[0]
    tm = min(512, n)
    assert n % tm == 0
    ncol = w_bf16.shape[1]
    if mode == "glu":
        outs = [(D_MODEL, F32)]
    elif mode == "peer":
        outs = [(ncol, F32), (D_MODEL, BF16)]
    elif mode == "kv":
        outs = [(ncol, F32), (ncol, BF16)]
    elif mode == "nsa":
        outs = [(D_MODEL, F32), (GATE_COLS, F32)]
    res = pl.pallas_call(
        functools.partial(_norm_matmul_kernel, mode=mode),
        grid=(n // tm,),
        in_specs=[pl.BlockSpec((tm, D_MODEL), lambda i: (i, 0)),
                  pl.BlockSpec((1, D_MODEL), lambda i: (0, 0)),
                  pl.BlockSpec((D_MODEL, ncol), lambda i: (0, 0))],
        out_specs=[pl.BlockSpec((tm, c), lambda i: (i, 0)) for c, _ in outs],
        out_shape=[jax.ShapeDtypeStruct((n, c), dt) for c, dt in outs],
        compiler_params=_params("parallel"),
        name="norm_matmul_" + mode,
    )(x, g.reshape(1, D_MODEL), w_bf16)
    return res


def _matmul_res_kernel(a_ref, w_ref, h_ref, o_ref):
    o_ref[...] = h_ref[...] + jnp.dot(a_ref[...].astype(BF16), w_ref[...], preferred_element_type=F32)


def _matmul_res(a, w_bf16, h):
    n = h.shape[0]
    tm = min(512, n)
    return pl.pallas_call(
        _matmul_res_kernel,
        grid=(n // tm,),
        in_specs=[pl.BlockSpec((tm, D_MODEL), lambda i: (i, 0)),
                  pl.BlockSpec((D_MODEL, D_MODEL), lambda i: (0, 0)),
                  pl.BlockSpec((tm, D_MODEL), lambda i: (i, 0))],
        out_specs=pl.BlockSpec((tm, D_MODEL), lambda i: (i, 0)),
        out_shape=jax.ShapeDtypeStruct((n, D_MODEL), F32),
        compiler_params=_params("parallel"),
        name="attn_out_proj",
    )(a, w_bf16, h)


def _conv_kernel(prev_ref, u_ref, h_ref, dwk_ref, dwb_ref, lng_ref, lnb_ref, wout_ref, o_ref,
                 win_ref, z_ref, *, tt, chunk):
    win_ref[0:CONV_HALO, :] = prev_ref[0, 0]
    win_ref[CONV_HALO:CONV_HALO + tt, :] = u_ref[0]
    first_tap = CONV_HALO - (CONV_WIDTH - 1)

    for r0 in range(0, tt, chunk):
        acc = jnp.zeros((chunk, D_MODEL), F32)
        for w in range(CONV_WIDTH):
            acc = acc + win_ref[r0 + first_tap + w:r0 + first_tap + w + chunk, :] * dwk_ref[w:w + 1, :]
        y = acc + dwb_ref[...]
        mu = jnp.mean(y, axis=-1, keepdims=True)
        yc = y - mu
        var = jnp.mean(yc * yc, axis=-1, keepdims=True)
        yn = yc * lax.rsqrt(var + EPS) * lng_ref[...] + lnb_ref[...]
        z_ref[r0:r0 + chunk, :] = yn * _sigmoid(yn)
    o_ref[0] = h_ref[0] + jnp.dot(z_ref[...].astype(BF16), wout_ref[...], preferred_element_type=F32)


def _conv_module(u, past, h, dw_k, dw_b, ln_g, ln_b, w_out_bf16):
    b, t, _ = u.shape
    tt = min(256, t)
    chunk = min(32, tt)
    nt = t // tt
    full = jnp.concatenate([jnp.zeros((b, CONV_HALO - (CONV_WIDTH - 1), D_MODEL), F32), past, u], axis=1)
    starts = np.arange(nt) * tt
    prev = jnp.stack([full[:, s:s + CONV_HALO] for s in starts], axis=1)
    vec = lambda a: a.reshape(1, D_MODEL)
    return pl.pallas_call(
        functools.partial(_conv_kernel, tt=tt, chunk=chunk),
        grid=(b, nt),
        in_specs=[pl.BlockSpec((1, 1, CONV_HALO, D_MODEL), lambda i, j: (i, j, 0, 0)),
                  pl.BlockSpec((1, tt, D_MODEL), lambda i, j: (i, j, 0)),
                  pl.BlockSpec((1, tt, D_MODEL), lambda i, j: (i, j, 0)),
                  pl.BlockSpec((CONV_WIDTH, D_MODEL), lambda i, j: (0, 0)),
                  pl.BlockSpec((1, D_MODEL), lambda i, j: (0, 0)),
                  pl.BlockSpec((1, D_MODEL), lambda i, j: (0, 0)),
                  pl.BlockSpec((1, D_MODEL), lambda i, j: (0, 0)),
                  pl.BlockSpec((D_MODEL, D_MODEL), lambda i, j: (0, 0))],
        out_specs=pl.BlockSpec((1, tt, D_MODEL), lambda i, j: (i, j, 0)),
        out_shape=jax.ShapeDtypeStruct((b, t, D_MODEL), F32),
        scratch_shapes=[pltpu.VMEM((CONV_HALO + tt, D_MODEL), F32), pltpu.VMEM((tt, D_MODEL), F32)],
        compiler_params=_params("parallel", "parallel"),
        name="conformer_conv",
    )(prev, u, h, dw_k, vec(dw_b), vec(ln_g), vec(ln_b), w_out_bf16)


BF16_ROWS = 16
SUBLANES = 8


def _batcher_pairs(lo, hi):
    def merge(lo, hi, r):
        step = r * 2
        if step < hi - lo:
            yield from merge(lo, hi, step)
            yield from merge(lo + r, hi, step)
            yield from ((i, i + r) for i in range(lo + r, hi - r, step))
        else:
            yield (lo, lo + r)

    if hi - lo >= 1:
        mid = lo + (hi - lo) // 2
        yield from _batcher_pairs(lo, mid)
        yield from _batcher_pairs(mid + 1, hi)
        yield from merge(lo, hi, 1)


def _compare_exchange(x, i, j):
    x[i], x[j] = jnp.maximum(x[i], x[j]), jnp.minimum(x[i], x[j])


def _merge_sublanes(x):
    n = len(x)
    for shift in (4, 2, 1):
        y = [pltpu.roll(v, shift, 0) for v in x]
        x = [jnp.maximum(x[i], y[n - 1 - i]) for i in range(n)]
        d = n // 2
        while d:
            for i in range(n):
                if not i & d:
                    _compare_exchange(x, i, i + d)
            d //= 2
    return x


def _top16_sorted(s):
    x = [s[SUBLANES * j:SUBLANES * (j + 1)] for j in range(s.shape[0] // SUBLANES)]
    for i, j in _batcher_pairs(0, len(x) - 1):
        _compare_exchange(x, i, j)
    return _merge_sublanes(x)


def _bf16_bits(x):
    return pltpu.bitcast(x.astype(BF16).astype(F32), jnp.uint32)


def _bf16_pair(x):
    hi = _bf16_bits(x)
    return hi | (hi >> 16)


def _bf16_pack_rows(x):
    r, c = x.shape
    x4 = x.reshape(r // 16, 2, 8, c)
    words = _bf16_bits(x4[:, 1]) | (_bf16_bits(x4[:, 0]) >> 16)
    return words.reshape(r // 2, c)


def _bf16_unpack_rows(words):
    lo = pltpu.bitcast(words << 16, F32)
    hi = pltpu.bitcast(words & jnp.uint32(0xFFFF0000), F32)
    return jnp.stack([lo, hi], axis=1)


def _peer_topk_kernel(q_ref, k1_ref, k2_ref, n1_ref, e1_ref, r2_ref, e2_ref):
    half = PEER_DKEY // 2
    k = PEER_TOPK
    q = q_ref[...]
    s1 = lax.dot_general(k1_ref[...], q[:, :half], _NT, preferred_element_type=F32)
    s2 = lax.dot_general(k2_ref[...], q[:, half:], _NT, preferred_element_type=F32)
    v1 = _top16_sorted(s1)
    v2 = _top16_sorted(s2)
    sub = lax.broadcasted_iota(jnp.int32, v1[0].shape, 0)
    a_lo, a_hi = v1[0], v1[SUBLANES]
    n_b = jnp.full(sub.shape, k, jnp.int32)
    for r in range(1, SUBLANES):
        a_lo = jnp.where(sub == r, v1[r], a_lo)
        a_hi = jnp.where(sub == r, v1[SUBLANES + r], a_hi)
        n_b = jnp.where(sub == r, k // (r + 1), n_b)
    lists = [jnp.where(n_b > b, a_lo + v2[b], -jnp.inf) for b in range(k)]
    single = a_hi + v2[0]
    merged, x = [], single
    for b in range(k):
        merged.append(jnp.maximum(lists[b], x))
        x = jnp.minimum(lists[b], x)
    best = _merge_sublanes(merged)
    cmax, tau = best[0], best[-1]
    z = functools.reduce(jnp.add, [jnp.exp(t - cmax) for t in best])
    count_lo = functools.reduce(jnp.add, [jnp.where(t >= tau, 1.0, 0.0) for t in lists])
    count_hi = jnp.where(single >= tau, 1.0, 0.0)
    n1 = jnp.zeros(s1.shape, F32)
    for a in reversed(range(k)):
        r = a % SUBLANES
        n_a = (count_lo if a < SUBLANES else count_hi)[r:r + 1]
        n1 = jnp.where(s1 == v1[a][0:1], n_a, n1)
    rank2 = functools.reduce(jnp.add, [jnp.where(v2[b][0:1] > s2, 1.0, 0.0) for b in range(k)])
    n1_ref[0] = _bf16_pair(n1)
    e1_ref[0] = _bf16_pair(jnp.exp(s1 - v1[0][0:1]) / z[0:1])
    r2_ref[0] = _bf16_pack_rows(rank2)
    e2_ref[0] = _bf16_pack_rows(jnp.exp(s2 - v2[0][0:1]))


def _peer_topk(q, sub_k):
    n = q.shape[0]
    tt = 256
    assert n % tt == 0
    row = pl.BlockSpec((1, PEER_KEYS, tt), lambda i, h: (h, 0, i))
    tile = pl.BlockSpec((1, PEER_KEYS // 2, tt), lambda i, h: (h, 0, i))
    row_shape = jax.ShapeDtypeStruct((PEER_HEADS, PEER_KEYS, n), jnp.uint32)
    tile_shape = jax.ShapeDtypeStruct((PEER_HEADS, PEER_KEYS // 2, n), jnp.uint32)
    return pl.pallas_call(
        _peer_topk_kernel,
        grid=(n // tt, PEER_HEADS),
        in_specs=[pl.BlockSpec((tt, PEER_DKEY), lambda i, h: (i, h)),
                  pl.BlockSpec((PEER_KEYS, PEER_DKEY // 2), lambda i, h: (0, 0)),
                  pl.BlockSpec((PEER_KEYS, PEER_DKEY // 2), lambda i, h: (0, 0))],
        out_specs=[row, row, tile, tile],
        out_shape=[row_shape, row_shape, tile_shape, tile_shape],
        compiler_params=_params("parallel", "parallel"),
        name="peer_topk",
    )(q, sub_k[0], sub_k[1])


PEER_HALF_I1 = 4
PEER_HALF = PEER_HALF_I1 * PEER_KEYS


def _peer_dense_kernel(xn_ref, u_ref, vta_ref, vtb_ref, n1_ref, e1_ref, r2_ref, e2_ref, h_ref, gout_ref,
                       o_ref, a_ref, wa_ref, wb_ref, acc_ref, *, tt, final_norm):
    e = pl.program_id(1)
    last = pl.num_programs(1) - 1

    @pl.when(e == 0)
    def _():
        acc_ref[...] = jnp.zeros_like(acc_ref)
        wb_ref[...] = jnp.zeros_like(wb_ref)

    live = jnp.where(e < last, jnp.uint32(1), jnp.uint32(0))
    n_tiles = PEER_KEYS // BF16_ROWS

    def packed_row(ref, hd, r, lanes, scale=None):
        word = ref[hd, r:r + 1, lanes]
        if scale is not None:
            word = word * scale
        return pltpu.bitcast(jnp.broadcast_to(word, (8, 128)), BF16)[None]

    def as_bf16_tiles(words):
        return pltpu.bitcast(words.reshape(n_tiles, 8, 128), BF16)

    def build(half, w_ref):
        rows_u = slice(half * PEER_HALF, (half + 1) * PEER_HALF)
        a_ref[...] = lax.dot_general(u_ref[rows_u, :], xn_ref[...], _NT, preferred_element_type=F32)
        for lg in range(tt // 128):
            lanes = slice(lg * 128, (lg + 1) * 128)
            for il in range(PEER_HALF_I1):
                r = half * PEER_HALF_I1 + il
                rows = slice(il * PEER_KEYS, (il + 1) * PEER_KEYS)
                gate = jnp.zeros((n_tiles, BF16_ROWS, 128), BF16)
                for hd in range(PEER_HEADS):
                    n1b = packed_row(n1_ref, hd, r, lanes, live)
                    e1b = packed_row(e1_ref, hd, r, lanes)
                    r2 = as_bf16_tiles(r2_ref[hd, :, lanes])
                    e2 = as_bf16_tiles(e2_ref[hd, :, lanes])
                    gate = gate + jnp.where(r2 < n1b, e2 * e1b, jnp.zeros_like(e2))
                gate = _bf16_unpack_rows(pltpu.bitcast(gate, jnp.uint32))
                act = _gelu_tanh(a_ref[rows, lanes]).reshape(n_tiles, 2, 8, 128)
                w_ref[rows, lanes] = (act * gate).reshape(PEER_KEYS, 128).astype(BF16)

    acc_ref[...] += jnp.dot(vtb_ref[...], wb_ref[...], preferred_element_type=F32)
    build(0, wa_ref)
    acc_ref[...] += jnp.dot(vta_ref[...], wa_ref[...], preferred_element_type=F32)
    build(1, wb_ref)

    @pl.when(e == last)
    def _():
        out = h_ref[...] + acc_ref[...].T
        if final_norm:
            out = _rms(out, gout_ref[...])
        o_ref[...] = out


def _peer_dense(xn_bf16, u_bf16, vt_bf16, n1, e1, r2, e2, h, g_out, final_norm):
    n = h.shape[0]
    tt = min(512, n)
    n_exp = u_bf16.shape[0]
    n_e = n_exp // (2 * PEER_HALF)
    tok = lambda i, e: (i, 0)
    cur = lambda e: jnp.minimum(e, n_e - 1)
    return pl.pallas_call(
        functools.partial(_peer_dense_kernel, tt=tt, final_norm=final_norm),
        grid=(n // tt, n_e + 1),
        in_specs=[pl.BlockSpec((tt, D_MODEL), tok),
                  pl.BlockSpec((2 * PEER_HALF, D_MODEL), lambda i, e: (cur(e), 0)),
                  pl.BlockSpec((D_MODEL, PEER_HALF), lambda i, e: (0, 2 * cur(e))),
                  pl.BlockSpec((D_MODEL, PEER_HALF), lambda i, e: (0, jnp.maximum(2 * e - 1, 0))),
                  pl.BlockSpec((PEER_HEADS, 2 * PEER_HALF_I1, tt), lambda i, e: (0, cur(e), i)),
                  pl.BlockSpec((PEER_HEADS, 2 * PEER_HALF_I1, tt), lambda i, e: (0, cur(e), i)),
                  pl.BlockSpec((PEER_HEADS, PEER_KEYS // 2, tt), lambda i, e: (0, 0, i)),
                  pl.BlockSpec((PEER_HEADS, PEER_KEYS // 2, tt), lambda i, e: (0, 0, i)),
                  pl.BlockSpec((tt, D_MODEL), tok),
                  pl.BlockSpec((1, D_MODEL), lambda i, e: (0, 0))],
        out_specs=pl.BlockSpec((tt, D_MODEL), tok),
        out_shape=jax.ShapeDtypeStruct((n, D_MODEL), F32),
        scratch_shapes=[pltpu.VMEM((PEER_HALF, tt), F32), pltpu.VMEM((PEER_HALF, tt), BF16),
                        pltpu.VMEM((PEER_HALF, tt), BF16), pltpu.VMEM((D_MODEL, tt), F32)],
        compiler_params=_params("parallel", "arbitrary"),
        name="peer_dense",
    )(xn_bf16, u_bf16, vt_bf16, vt_bf16, n1, e1, r2, e2, h, g_out.reshape(1, D_MODEL))


def _transpose_kernel(v_ref, o_ref):
    o_ref[...] = v_ref[...].T.astype(o_ref.dtype)


def _transpose_bf16(v):
    r, c = v.shape
    tr = 512
    return pl.pallas_call(
        _transpose_kernel,
        grid=(r // tr,),
        in_specs=[pl.BlockSpec((tr, c), lambda i: (i, 0))],
        out_specs=pl.BlockSpec((c, tr), lambda i: (0, i)),
        out_shape=jax.ShapeDtypeStruct((c, r), BF16),
        compiler_params=_params("parallel"),
        name="transpose_bf16",
    )(v)


def _peer_layer(h, g_norm, w_q, sub_k, u_emb, v_emb, g_out, final_norm):
    q, xn = _norm_matmul(h, g_norm, w_q.astype(BF16), "peer")
    n1, e1, r2, e2 = _peer_topk(q, sub_k)
    return _peer_dense(xn, u_emb.astype(BF16), _transpose_bf16(v_emb), n1, e1, r2, e2, h, g_out, final_norm)


GATHER_PAGES = 8
GATHER_ROWS = GATHER_PAGES * PAGE_SIZE


def _gather_kernel(pt_ref, *refs, n_steps):
    page_refs, new_ref, o_ref = refs[:GATHER_PAGES], refs[GATHER_PAGES], refs[GATHER_PAGES + 1]
    p = pl.program_id(1)

    @pl.when(p < n_steps)
    def _():
        for k, page_ref in enumerate(page_refs):
            o_ref[0, k * PAGE_SIZE:(k + 1) * PAGE_SIZE, :] = page_ref[0].T.astype(o_ref.dtype)

    @pl.when(p >= n_steps)
    def _():
        o_ref[0] = new_ref[0].astype(o_ref.dtype)


def _gather_pages(cache, page_table, new_rows, out_dtype):
    b, n_pages = page_table.shape
    assert n_pages % GATHER_PAGES == 0 and new_rows.shape[1] <= GATHER_ROWS
    n_steps = n_pages // GATHER_PAGES
    new_pad = jnp.pad(new_rows, ((0, 0), (0, GATHER_ROWS - new_rows.shape[1]), (0, 0)))

    def page_spec(k):
        return pl.BlockSpec((1, KV_COLS, PAGE_SIZE),
                            lambda i, p, pt: (pt[i, jnp.minimum(p, n_steps - 1) * GATHER_PAGES + k], 0, 0))

    grid_spec = pltpu.PrefetchScalarGridSpec(
        num_scalar_prefetch=1,
        grid=(b, n_steps + 1),
        in_specs=[page_spec(k) for k in range(GATHER_PAGES)]
        + [pl.BlockSpec((1, GATHER_ROWS, KV_COLS), lambda i, p, pt: (i, 0, 0))],
        out_specs=pl.BlockSpec((1, GATHER_ROWS, KV_COLS), lambda i, p, pt: (i, p, 0)),
    )
    return pl.pallas_call(
        functools.partial(_gather_kernel, n_steps=n_steps),
        grid_spec=grid_spec,
        out_shape=jax.ShapeDtypeStruct((b, (n_steps + 1) * GATHER_ROWS, KV_COLS), out_dtype),
        compiler_params=_params("parallel", "arbitrary"),
        name="gather_pages",
    )(page_table, *([cache] * GATHER_PAGES), new_pad)


CMP_TILE = 128


def _compress_kernel(main_ref, halo_ref, pe_ref, w1k_ref, w1v_ref, w2k_ref, w2v_ref, o_ref, win_ref):
    n_tok = CMP_TILE * CMP_STRIDE
    n_lane_tiles = KV_COLS // 128
    for c in range(n_lane_tiles):
        win_ref[c, 0:n_tok, :] = main_ref[0, :, c * 128:(c + 1) * 128]
        win_ref[c, n_tok:n_tok + CMP_STRIDE, :] = halo_ref[0, :, c * 128:(c + 1) * 128]
    hk = jnp.zeros((CMP_TILE, N_KV_HEADS * CMP_HIDDEN), F32)
    hv = jnp.zeros((CMP_TILE, N_KV_HEADS * CMP_HIDDEN), F32)
    for s in range(CMP_BLOCK):
        x = jnp.concatenate([win_ref[c, pl.ds(s, CMP_TILE, stride=CMP_STRIDE), :] for c in range(n_lane_tiles)],
                            axis=1)
        x = (x + pe_ref[s]).astype(BF16)
        hk = hk + jnp.dot(x[:, :K_COLS], w1k_ref[s], preferred_element_type=F32)
        hv = hv + jnp.dot(x[:, K_COLS:], w1v_ref[s], preferred_element_type=F32)
    o_ref[0, :, :K_COLS] = jnp.dot(_gelu_tanh(hk).astype(BF16), w2k_ref[...], preferred_element_type=F32)
    o_ref[0, :, K_COLS:] = jnp.dot(_gelu_tanh(hv).astype(BF16), w2v_ref[...], preferred_element_type=F32)


def _block_diag(w):
    eye = jnp.eye(N_KV_HEADS, dtype=w.dtype)
    out = jnp.einsum("gh,...ab->...gahb", eye, w)
    return out.reshape(w.shape[:-2] + (N_KV_HEADS * w.shape[-2], N_KV_HEADS * w.shape[-1]))


def _compress(rows, n_tiles, cmp_pe, cmp_w1, cmp_w2):
    b, length = rows.shape[:2]
    n_tok = CMP_TILE * CMP_STRIDE
    last_halo = length // CMP_STRIDE - 1
    w1 = cmp_w1.reshape(2, CMP_BLOCK, HEAD_DIM, CMP_HIDDEN)
    w1k, w1v = _block_diag(w1[0]).astype(BF16), _block_diag(w1[1]).astype(BF16)
    w2k, w2v = _block_diag(cmp_w2[0]).astype(BF16), _block_diag(cmp_w2[1]).astype(BF16)
    pe = jnp.concatenate([jnp.tile(cmp_pe[0], (1, N_KV_HEADS)), jnp.tile(cmp_pe[1], (1, N_KV_HEADS))], axis=1)
    pe = pe.reshape(CMP_BLOCK, 1, KV_COLS)
    const3 = lambda i, j: (0, 0, 0)
    return pl.pallas_call(
        _compress_kernel,
        grid=(b, n_tiles),
        in_specs=[pl.BlockSpec((1, n_tok, KV_COLS), lambda i, j: (i, j, 0)),
                  pl.BlockSpec((1, CMP_STRIDE, KV_COLS),
                               lambda i, j: (i, jnp.minimum((j + 1) * CMP_TILE, last_halo), 0)),
                  pl.BlockSpec((CMP_BLOCK, 1, KV_COLS), const3),
                  pl.BlockSpec((CMP_BLOCK, K_COLS, N_KV_HEADS * CMP_HIDDEN), const3),
                  pl.BlockSpec((CMP_BLOCK, K_COLS, N_KV_HEADS * CMP_HIDDEN), const3),
                  pl.BlockSpec((N_KV_HEADS * CMP_HIDDEN, K_COLS), lambda i, j: (0, 0)),
                  pl.BlockSpec((N_KV_HEADS * CMP_HIDDEN, K_COLS), lambda i, j: (0, 0))],
        out_specs=pl.BlockSpec((1, CMP_TILE, KV_COLS), lambda i, j: (i, j, 0)),
        out_shape=jax.ShapeDtypeStruct((b, n_tiles * CMP_TILE, KV_COLS), F32),
        scratch_shapes=[pltpu.VMEM((KV_COLS // 128, n_tok + CMP_STRIDE, 128), F32)],
        compiler_params=_params("parallel", "parallel"),
        name="cmp_compress",
    )(rows, rows, pe, w1k, w1v, w2k, w2v)


def _head_perm():
    c = np.arange(D_MODEL)
    h, g, d = c // K_COLS, (c % K_COLS) // HEAD_DIM, c % HEAD_DIM
    return (g * HEADS_PER_KV + h) * HEAD_DIM + d


def _gate_expand(branch):
    c = np.arange(D_MODEL)
    h, g = c // K_COLS, (c % K_COLS) // HEAD_DIM
    m = np.zeros((GATE_COLS, D_MODEL), np.float32)
    m[(g * HEADS_PER_KV + h) * N_BRANCH + branch, c] = 1.0
    return jnp.asarray(m)


def _block_diag_queries(q):
    tq = q.shape[0]
    lane_g = lax.broadcasted_iota(jnp.int32, (tq, K_COLS), 1) // HEAD_DIM
    blocks = []
    for g in range(N_KV_HEADS):
        for h in range(HEADS_PER_KV):
            blocks.append(jnp.where(lane_g == g, q[:, h * K_COLS:(h + 1) * K_COLS], 0.0))
    return jnp.concatenate(blocks, axis=0)


def _diag_heads(acc, tq):
    lane_g = lax.broadcasted_iota(jnp.int32, (tq, K_COLS), 1) // HEAD_DIM
    outs = []
    for h in range(HEADS_PER_KV):
        o = jnp.zeros((tq, K_COLS), F32)
        for g in range(N_KV_HEADS):
            r0 = (g * HEADS_PER_KV + h) * tq
            o = o + jnp.where(lane_g == g, acc[r0:r0 + tq], 0.0)
        outs.append(o)
    return jnp.concatenate(outs, axis=1)


_NT = (((1,), (1,)), ((), ()))


def _cmp_attn_kernel(q_ref, gate_ref, kvc_ref, ovt_ref, eg_ref, oc_ref, sel_ref, sc_ref,
                     *, tq, pos0, n_slc_loop):
    q0 = pos0 + pl.program_id(1) * tq
    n_pad = kvc_ref.shape[1]
    s_pad = ovt_ref.shape[0]
    rows = N_HEADS * tq
    qbd = _block_diag_queries(q_ref[0]).astype(BF16)
    kc = kvc_ref[0, :, :K_COLS].astype(BF16)
    vc = kvc_ref[0, :, K_COLS:].astype(BF16)
    s = lax.dot_general(qbd, kc, _NT, preferred_element_type=F32)
    t_row = q0 + (lax.broadcasted_iota(jnp.int32, (rows, n_pad), 0) & (tq - 1))
    n_col = lax.broadcasted_iota(jnp.int32, (rows, n_pad), 1)
    vis = (n_col * CMP_STRIDE + (CMP_BLOCK - 1)) <= t_row
    m = jnp.max(jnp.where(vis, s, -jnp.inf), axis=-1, keepdims=True)
    m = jnp.where(m > -jnp.inf, m, 0.0)
    e = jnp.where(vis, jnp.exp(s - m), 0.0)
    p = e / jnp.maximum(jnp.sum(e, axis=-1, keepdims=True), 1e-30)
    oc = _diag_heads(jnp.dot(p.astype(BF16), vc, preferred_element_type=F32), tq)
    oc_ref[0] = oc * jnp.dot(gate_ref[0], eg_ref[...], preferred_element_type=F32)

    psum = jnp.concatenate(
        [sum(p[(g * HEADS_PER_KV + h) * tq:(g * HEADS_PER_KV + h + 1) * tq] for h in range(HEADS_PER_KV))
         for g in range(N_KV_HEADS)], axis=0)
    imp_t = lax.dot_general(ovt_ref[...], psum, _NT, preferred_element_type=F32)
    cols = N_KV_HEADS * tq
    blk = lax.broadcasted_iota(jnp.int32, (s_pad, cols), 0)
    cur = (q0 + (lax.broadcasted_iota(jnp.int32, (s_pad, cols), 1) & (tq - 1))) // SLC_BLOCK
    forced = (blk == 0) | (blk == cur) | (blk == cur - 1)
    score = jnp.where(forced, jnp.inf, jnp.where(blk <= cur, imp_t, -jnp.inf))
    sc_ref[...] = score

    score = score[:n_slc_loop]
    blk = lax.broadcasted_iota(jnp.int32, (n_slc_loop, cols), 0)
    cur = (q0 + (lax.broadcasted_iota(jnp.int32, (n_slc_loop, cols), 1) & (tq - 1))) // SLC_BLOCK

    def rank_body(jb, rank):
        rows8 = sc_ref[pl.ds(pl.multiple_of(jb * 8, 8), 8), :]
        for r in range(8):
            row = rows8[r:r + 1]
            tie = jnp.where(row == score, jnp.where(jb * 8 + r < blk, 1.0, 0.0), 0.0)
            rank = rank + jnp.where(row > score, 1.0, tie)
        return rank

    rank = lax.fori_loop(0, n_slc_loop // 8, rank_body, jnp.zeros((n_slc_loop, cols), F32))
    sel_t = jnp.where(rank < float(N_SELECT), jnp.where(blk <= cur, 1.0, 0.0), 0.0)
    if n_slc_loop < s_pad:
        sel_t = jnp.concatenate([sel_t, jnp.zeros((s_pad - n_slc_loop, cols), F32)], axis=0)
    sel_t = sel_t.astype(BF16)
    ci = lax.broadcasted_iota(jnp.int32, (cols, cols), 0)
    cj = lax.broadcasted_iota(jnp.int32, (cols, cols), 1)
    eye = jnp.where(ci == cj, 1.0, 0.0).astype(BF16)
    sel_ref[0, 0] = lax.dot_general(eye, sel_t, _NT, preferred_element_type=F32)


def _cmp_attn(q, gates, kvc, ov_t, pos0, tq, n_slc):
    b, t, _ = q.shape
    s_pad = ov_t.shape[0]
    n_pad = kvc.shape[1]
    n_slc_loop = -(-n_slc // 8) * 8
    return pl.pallas_call(
        functools.partial(_cmp_attn_kernel, tq=tq, pos0=pos0, n_slc_loop=n_slc_loop),
        grid=(b, t // tq),
        in_specs=[pl.BlockSpec((1, tq, D_MODEL), lambda i, j: (i, j, 0)),
                  pl.BlockSpec((1, tq, GATE_COLS), lambda i, j: (i, j, 0)),
                  pl.BlockSpec((1, n_pad, KV_COLS), lambda i, j: (i, 0, 0)),
                  pl.BlockSpec((s_pad, n_pad), lambda i, j: (0, 0)),
                  pl.BlockSpec((GATE_COLS, D_MODEL), lambda i, j: (0, 0))],
        out_specs=[pl.BlockSpec((1, tq, D_MODEL), lambda i, j: (i, j, 0)),
                   pl.BlockSpec((1, 1, N_KV_HEADS * tq, s_pad), lambda i, j: (i, j, 0, 0))],
        out_shape=[jax.ShapeDtypeStruct((b, t, D_MODEL), F32),
                   jax.ShapeDtypeStruct((b, t // tq, N_KV_HEADS * tq, s_pad), F32)],
        scratch_shapes=[pltpu.VMEM((s_pad, N_KV_HEADS * tq), F32)],
        compiler_params=_params("parallel", "parallel"),
        name="nsa_cmp_select",
    )(q, gates, kvc, ov_t, _gate_expand(0))


WIN_KEYS = 768
ATTN_ROWS = 64


def _lane_tiles(x):
    return [x[:, i * 128:(i + 1) * 128] for i in range(x.shape[1] // 128)]


def _row_max(x):
    return jnp.max(functools.reduce(jnp.maximum, _lane_tiles(x)), axis=-1, keepdims=True)


def _row_sum(x):
    return jnp.sum(functools.reduce(jnp.add, _lane_tiles(x)), axis=-1, keepdims=True)


def _sparse_attn_kernel(q_ref, gate_ref, sel_ref, oc_ref, kslc_ref, kwin_ref, egs_ref, egw_ref, o_ref,
                        qaug_ref, s_ref, p_ref, sw_ref, pw_ref, m_ref, l_ref, alpha_ref, acc_ref,
                        *, tq, tk, pos0, win_pos0):
    q0 = pos0 + pl.program_id(1) * tq
    rows = N_HEADS * tq
    s_pad = sel_ref.shape[3]
    rb = min(ATTN_ROWS, rows)

    qaug_ref[:, 0:K_COLS] = _block_diag_queries(q_ref[0]).astype(BF16)
    not_sel = sel_ref[0, 0] - 1.0
    qaug_ref[:, K_COLS:] = jnp.concatenate(
        [not_sel[g * tq:(g + 1) * tq] for g in range(N_KV_HEADS) for _ in range(HEADS_PER_KV)],
        axis=0).astype(BF16)

    def q_pos(r0, width):
        return q0 + ((r0 + lax.broadcasted_iota(jnp.int32, (rb, width), 0)) & (tq - 1))

    m_ref[...] = jnp.full_like(m_ref, NEG_BIG)
    l_ref[...] = jnp.zeros_like(l_ref)
    acc_ref[...] = jnp.zeros_like(acc_ref)
    blk_lane = lax.broadcasted_iota(jnp.int32, (tk, s_pad), 1)
    blk_key = lax.broadcasted_iota(jnp.int32, (tk, s_pad), 0) // SLC_BLOCK
    k_col = lax.broadcasted_iota(jnp.int32, (rb, tk), 1)

    n_split = s_ref.shape[0]
    part = rows // n_split

    def slc_tile(j, causal):
        ks = pl.ds(pl.multiple_of(j * tk, tk), tk)
        onehot = jnp.where(blk_lane == j * (tk // SLC_BLOCK) + blk_key, -NEG_BIG, 0.0).astype(BF16)
        kaug = jnp.concatenate([kslc_ref[0, ks, 0:K_COLS], onehot], axis=1)
        vt = kslc_ref[0, ks, K_COLS:KV_COLS]
        for sp in range(n_split):
            s_ref[sp] = lax.dot_general(qaug_ref[sp * part:(sp + 1) * part, :], kaug, _NT,
                                        preferred_element_type=F32)
        for sp in range(n_split):
            for r0 in range(0, part, rb):
                blk = slice(sp * part + r0, sp * part + r0 + rb)
                s = s_ref[sp, r0:r0 + rb, :]
                if causal:
                    s = jnp.where(j * tk + k_col <= q_pos(sp * part + r0, tk), s, NEG_BIG)
                m_old = m_ref[blk, :]
                m_new = jnp.maximum(m_old, _row_max(s))
                p = jnp.exp(s - m_new)
                alpha = jnp.exp(m_old - m_new)
                l_ref[blk, :] = alpha * l_ref[blk, :] + _row_sum(p)
                m_ref[blk, :] = m_new
                alpha_ref[blk, :] = alpha
                p_ref[sp, r0:r0 + rb, :] = p.astype(BF16)
            prows = slice(sp * part, (sp + 1) * part)
            acc_ref[prows, :] = alpha_ref[prows, :] * acc_ref[prows, :] + jnp.dot(
                p_ref[sp], vt, preferred_element_type=F32)

    last_tile = (q0 + tq - 1) // tk
    lax.fori_loop(0, last_tile, lambda j, c: (slc_tile(j, False), c)[1], 0)
    slc_tile(last_tile, True)
    o_s = _diag_heads(acc_ref[...] / jnp.maximum(l_ref[...], 1e-30), tq)

    lw = kwin_ref.shape[1]
    w0 = jnp.clip((q0 - win_pos0 - WINDOW) // 128 * 128, 0, lw - WIN_KEYS)
    wk = pl.ds(pl.multiple_of(w0, 128), WIN_KEYS)
    sw_ref[...] = lax.dot_general(qaug_ref[:, 0:K_COLS], kwin_ref[0, wk, 0:K_COLS], _NT,
                                  preferred_element_type=F32)
    kpos = win_pos0 + w0 + lax.broadcasted_iota(jnp.int32, (rb, WIN_KEYS), 1)
    for r0 in range(0, rows, rb):
        blk = slice(r0, r0 + rb)
        t = q_pos(r0, WIN_KEYS)
        s = jnp.where((kpos <= t) & (kpos > t - WINDOW), sw_ref[blk, :], NEG_BIG)
        p = jnp.exp(s - _row_max(s))
        l_ref[blk, :] = _row_sum(p)
        pw_ref[blk, :] = p.astype(BF16)
    o_w = jnp.dot(pw_ref[...], kwin_ref[0, wk, K_COLS:KV_COLS], preferred_element_type=F32)
    o_w = o_w / jnp.maximum(l_ref[...], 1e-30)
    o_w = _diag_heads(o_w, tq)

    gate = gate_ref[0]
    g_s = jnp.dot(gate, egs_ref[...], preferred_element_type=F32)
    g_w = jnp.dot(gate, egw_ref[...], preferred_element_type=F32)
    o_ref[0] = (oc_ref[0] + g_s * o_s + g_w * o_w).astype(o_ref.dtype)


def _sparse_attn(q, gates, sel, oc, kslc, kwin, slc_col, win_col, pos0, win_pos0, tq, tk):
    b, t, _ = q.shape
    s_pad = sel.shape[3]
    ls, lw = kslc.shape[1], kwin.shape[1]
    assert ls % tk == 0 and lw % 128 == 0 and lw >= WIN_KEYS and tk % tq == 0
    rows = N_HEADS * tq
    n_split = max(1, min(4, rows // (2 * ATTN_ROWS)))
    qtile = lambda i, j: (i, j, 0)
    return pl.pallas_call(
        functools.partial(_sparse_attn_kernel, tq=tq, tk=tk, pos0=pos0, win_pos0=win_pos0),
        grid=(b, t // tq),
        in_specs=[pl.BlockSpec((1, tq, D_MODEL), qtile),
                  pl.BlockSpec((1, tq, GATE_COLS), qtile),
                  pl.BlockSpec((1, 1, N_KV_HEADS * tq, s_pad), lambda i, j: (i, j, 0, 0)),
                  pl.BlockSpec((1, tq, D_MODEL), qtile),
                  pl.BlockSpec((1, ls, KV_COLS), lambda i, j: (i, 0, slc_col)),
                  pl.BlockSpec((1, lw, KV_COLS), lambda i, j: (i, 0, win_col)),
                  pl.BlockSpec((GATE_COLS, D_MODEL), lambda i, j: (0, 0)),
                  pl.BlockSpec((GATE_COLS, D_MODEL), lambda i, j: (0, 0))],
        out_specs=pl.BlockSpec((1, tq, D_MODEL), qtile),
        out_shape=jax.ShapeDtypeStruct((b, t, D_MODEL), F32),
        scratch_shapes=[pltpu.VMEM((rows, K_COLS + s_pad), BF16),
                        pltpu.VMEM((n_split, rows // n_split, tk), F32),
                        pltpu.VMEM((n_split, rows // n_split, tk), BF16),
                        pltpu.VMEM((rows, WIN_KEYS), F32), pltpu.VMEM((rows, WIN_KEYS), BF16),
                        pltpu.VMEM((rows, 1), F32), pltpu.VMEM((rows, 1), F32), pltpu.VMEM((rows, 1), F32),
                        pltpu.VMEM((rows, K_COLS), F32)],
        compiler_params=_params("parallel", "arbitrary"),
        name="nsa_select_window",
    )(q, gates, sel, oc, kslc, kwin, _gate_expand(1), _gate_expand(2))


ATTN_TK = 512


def _overlap_t(s_pad, n_pad, n_slc, n_cmp):
    n = np.arange(n_pad)[None, :]
    s = np.arange(s_pad)[:, None]
    ov = ((n * CMP_STRIDE < s * SLC_BLOCK + SLC_BLOCK) & (n * CMP_STRIDE + CMP_BLOCK - 1 >= s * SLC_BLOCK)
          & (n < n_cmp) & (s < n_slc))
    return jnp.asarray(ov.astype(np.float32))


def _trunk(x, conv_past, past, pos0, p):
    b, t, _ = x.shape
    n = b * t
    kvh = (2, N_KV_HEADS, HEAD_DIM)
    h = x.reshape(n, D_MODEL)

    (u,) = _norm_matmul(h, p["norm_mix_g"][0], p["conv_w_in"][0].astype(BF16), "glu")
    u = u.reshape(b, t, D_MODEL)
    h = _conv_module(u, conv_past, h.reshape(b, t, D_MODEL), p["conv_dw_k"][0], p["conv_dw_b"][0],
                     p["conv_ln_g"][0], p["conv_ln_b"][0], p["conv_w_out"][0].astype(BF16)).reshape(n, D_MODEL)
    conv_state = jnp.concatenate([conv_past, u], axis=1)[:, -(CONV_WIDTH - 1):][None]
    h = _peer_layer(h, p["norm_ffn_g"][0], p["peer_w_q"][0], p["peer_sub_k"][0], p["peer_u"][0],
                    p["peer_v"][0], p["norm_out_g"], False)

    kv, kv_bf16 = _norm_matmul(h, p["norm_kv_g"], p["nsa_w_kv"].astype(BF16), "kv")
    kv = kv.reshape(b, t, N_BRANCH * KV_COLS)
    kv_bf16 = kv_bf16.reshape(b, t, N_BRANCH * KV_COLS)
    new_cmp, new_slc, new_win = (kv[..., i * KV_COLS:(i + 1) * KV_COLS] for i in range(N_BRANCH))
    if past is None:
        length, win_pos0 = t, pos0
        cmp_rows, k_slc, k_win, slc_col, win_col = kv, kv_bf16, kv_bf16, 1, 2
        win_all = new_win
    else:
        cache_cmp, cache_slc, cache_win, page_table = past
        length, win_pos0 = pos0 + t, pos0 - cache_win.shape[1]
        cmp_rows = _gather_pages(cache_cmp, page_table, new_cmp, F32)
        k_slc = _gather_pages(cache_slc, page_table, new_slc, BF16)
        win_all = jnp.concatenate([cache_win, new_win], axis=1)
        lw = max(WIN_KEYS, -(-win_all.shape[1] // 128) * 128)
        k_win = jnp.pad(win_all, ((0, 0), (0, lw - win_all.shape[1]), (0, 0))).astype(BF16)
        slc_col, win_col = 0, 0
    n_cmp = (length - CMP_BLOCK) // CMP_STRIDE + 1
    n_slc = -(-length // SLC_BLOCK)
    n_tiles = -(-n_cmp // CMP_TILE)
    kvc = _compress(cmp_rows, n_tiles, p["nsa_cmp_pe"], p["nsa_cmp_w1"], p["nsa_cmp_w2"])

    perm = _head_perm()
    w_in = p["nsa_w_in"][0]
    w_in = jnp.concatenate([w_in[:, perm], w_in[:, D_MODEL:],
                            jnp.zeros((D_MODEL, GATE_COLS - N_HEADS * N_BRANCH), F32)], axis=1).astype(BF16)
    q, gates = _norm_matmul(h, p["norm_mix_g"][1], w_in, "nsa")
    q = q.reshape(b, t, D_MODEL)
    gates = gates.reshape(b, t, GATE_COLS)
    tq = min(64, t)
    s_pad = -(-n_slc // 128) * 128
    oc, sel = _cmp_attn(q, gates, kvc, _overlap_t(s_pad, kvc.shape[1], n_slc, n_cmp), pos0, tq, n_slc)
    o = _sparse_attn(q, gates, sel, oc, k_slc, k_win, slc_col, win_col, pos0, win_pos0, tq, ATTN_TK)
    h = _matmul_res(o.reshape(n, D_MODEL), p["nsa_w_o"][0][perm, :].astype(BF16), h)
    y = _peer_layer(h, p["norm_ffn_g"][1], p["peer_w_q"][1], p["peer_sub_k"][1], p["peer_u"][1],
                    p["peer_v"][1], p["norm_out_g"], True)
    keep = min(WINDOW, win_all.shape[1])
    shape5 = lambda a: a.reshape(a.shape[:2] + kvh)
    return (y.reshape(b, t, D_MODEL), conv_state, shape5(new_cmp), shape5(new_slc), shape5(win_all[:, -keep:]))


def kernel(x_prompt, x_sample, state_conv, cache_cmp_kv, cache_slc_kv, cache_win_kv, page_table, norm_mix_g, norm_ffn_g, norm_kv_g, norm_out_g, conv_w_in, conv_dw_k, conv_dw_b, conv_ln_g, conv_ln_b, conv_w_out, nsa_w_kv, nsa_cmp_pe, nsa_cmp_w1, nsa_cmp_w2, nsa_w_in, nsa_w_o, peer_w_q, peer_sub_k, peer_u, peer_v):
    p = dict(norm_mix_g=norm_mix_g, norm_ffn_g=norm_ffn_g, norm_kv_g=norm_kv_g, norm_out_g=norm_out_g,
             conv_w_in=conv_w_in, conv_dw_k=conv_dw_k, conv_dw_b=conv_dw_b, conv_ln_g=conv_ln_g,
             conv_ln_b=conv_ln_b, conv_w_out=conv_w_out, nsa_w_kv=nsa_w_kv, nsa_cmp_pe=nsa_cmp_pe,
             nsa_cmp_w1=nsa_cmp_w1, nsa_cmp_w2=nsa_cmp_w2, nsa_w_in=nsa_w_in, nsa_w_o=nsa_w_o,
             peer_w_q=peer_w_q, peer_sub_k=peer_sub_k, peer_u=peer_u, peer_v=peer_v)
    bsz = x_prompt.shape[0]
    conv0 = jnp.zeros((bsz, CONV_WIDTH - 1, D_MODEL), x_prompt.dtype)
    y_p, conv_p, cmp_p, slc_p, win_p = _trunk(x_prompt, conv0, None, 0, p)
    dec_b, n_pages = page_table.shape
    flat = lambda c: c.reshape(c.shape[0], c.shape[1], KV_COLS)
    paged = lambda c: c.transpose(0, 2, 3, 4, 1).reshape(c.shape[0], KV_COLS, c.shape[1])
    past = (paged(cache_cmp_kv), paged(cache_slc_kv), flat(cache_win_kv), page_table)
    y_s, conv_s, cmp_s, slc_s, win_s = _trunk(x_sample, state_conv[0], past, n_pages * PAGE_SIZE, p)
    return (y_p, y_s, conv_p, cmp_p, slc_p, win_p, conv_s, cmp_s, slc_s, win_s)
```

```python
import functools
import math

import numpy as np
import jax
import jax.numpy as jnp
from jax import lax
from jax.experimental import pallas as pl
from jax.experimental.pallas import tpu as pltpu

F32 = jnp.float32
BF16 = jnp.bfloat16

D_MODEL = 1024
EPS = 1e-6
CONV_WIDTH = 31
CONV_HALO = 32
CONV_PAD = 8
SUBLANES = 8
N_HEADS = 16
HEAD_DIM = 64
N_KV_HEADS = 4
HEADS_PER_KV = 4
N_BRANCH = 3
KV_COLS = 2 * N_KV_HEADS * HEAD_DIM
K_COLS = N_KV_HEADS * HEAD_DIM
CMP_BLOCK = 32
CMP_STRIDE = 16
CMP_HIDDEN = 128
SLC_BLOCK = 64
N_SELECT = 16
WINDOW = 512
PAGE_SIZE = 128
PEER_HEADS = 8
PEER_KEYS = 128
PEER_TOPK = 16
PEER_DKEY = 256
GATE_COLS = 128

VMEM_LIMIT_BYTES = 56 * 1024 * 1024
NEG_BIG = -1e30


def _params(*sem):
    return pltpu.CompilerParams(dimension_semantics=sem, vmem_limit_bytes=VMEM_LIMIT_BYTES)


def _gelu_tanh(x):
    return 0.5 * x * (1.0 + jnp.tanh(0.7978845608028654 * (x + 0.044715 * (x * x * x))))


def _gelu_tanh_doubled(x):
    inner = x * (0.7978845608028654 * 0.044715 * (x * x) + 0.7978845608028654)
    return x * jnp.tanh(inner) + x


def _sigmoid(x):
    return 1.0 / (1.0 + jnp.exp(-x))


def _rms(x, g):
    return x * lax.rsqrt(jnp.mean(x * x, axis=-1, keepdims=True) + EPS) * g


def _norm_matmul_kernel(x_ref, g_ref, w_ref, *out_refs, mode):
    xn = _rms(x_ref[...], g_ref[...])
    xb = xn.astype(BF16)
    y = jnp.dot(xb, w_ref[...], preferred_element_type=F32)
    if mode == "glu":
        out_refs[0][...] = y[:, :D_MODEL] * _sigmoid(y[:, D_MODEL:])
    elif mode == "peer":
        out_refs[0][...] = y
        out_refs[1][...] = xb
    elif mode == "kv":
        out_refs[0][...] = y
        out_refs[1][...] = y.astype(BF16)
    elif mode == "nsa":
        out_refs[0][...] = y[:, :D_MODEL] * (HEAD_DIM ** -0.5)
        out_refs[1][...] = _sigmoid(y[:, D_MODEL:])
    else:
        raise ValueError(mode)


def _norm_matmul(x, g, w_bf16, mode):
    n = x.shape[0]
    tm = min(512, n)
    assert n % tm == 0
    ncol = w_bf16.shape[1]
    if mode == "glu":
        outs = [(D_MODEL, F32)]
    elif mode == "peer":
        outs = [(ncol, F32), (D_MODEL, BF16)]
    elif mode == "kv":
        outs = [(ncol, F32), (ncol, BF16)]
    elif mode == "nsa":
        outs = [(D_MODEL, F32), (GATE_COLS, F32)]
    res = pl.pallas_call(
        functools.partial(_norm_matmul_kernel, mode=mode),
        grid=(n // tm,),
        in_specs=[pl.BlockSpec((tm, D_MODEL), lambda i: (i, 0)),
                  pl.BlockSpec((1, D_MODEL), lambda i: (0, 0)),
                  pl.BlockSpec((D_MODEL, ncol), lambda i: (0, 0))],
        out_specs=[pl.BlockSpec((tm, c), lambda i: (i, 0)) for c, _ in outs],
        out_shape=[jax.ShapeDtypeStruct((n, c), dt) for c, dt in outs],
        compiler_params=_params("parallel"),
        name="norm_matmul_" + mode,
    )(x, g.reshape(1, D_MODEL), w_bf16)
    return res


def _matmul_res_kernel(a_ref, w_ref, h_ref, o_ref):
    o_ref[...] = h_ref[...] + jnp.dot(a_ref[...].astype(BF16), w_ref[...], preferred_element_type=F32)


def _matmul_res(a, w_bf16, h):
    n = h.shape[0]
    tm = min(512, n)
    return pl.pallas_call(
        _matmul_res_kernel,
        grid=(n // tm,),
        in_specs=[pl.BlockSpec((tm, D_MODEL), lambda i: (i, 0)),
                  pl.BlockSpec((D_MODEL, D_MODEL), lambda i: (0, 0)),
                  pl.BlockSpec((tm, D_MODEL), lambda i: (i, 0))],
        out_specs=pl.BlockSpec((tm, D_MODEL), lambda i: (i, 0)),
        out_shape=jax.ShapeDtypeStruct((n, D_MODEL), F32),
        compiler_params=_params("parallel"),
        name="attn_out_proj",
    )(a, w_bf16, h)


def _conv_kernel(prev_ref, u_ref, h_ref, dwk_ref, dwb_ref, lng_ref, lnb_ref, wout_ref, o_ref,
                 win_ref, zr_ref, z_ref, *, tt, chunk):
    win_ref[0:CONV_HALO, :] = prev_ref[0, 0]
    win_ref[CONV_HALO:CONV_HALO + tt, :] = u_ref[0]
    first_tap = CONV_HALO - (CONV_WIDTH - 1)

    win_ref[CONV_HALO + tt:, :] = jnp.zeros((CONV_PAD, D_MODEL), F32)
    n_z = tt + SUBLANES
    for r in range(SUBLANES):
        taps = [j for j in range(r, first_tap + CONV_WIDTH, SUBLANES) if j >= first_tap]
        for s0 in range(0, n_z, chunk):
            n = min(chunk, n_z - s0)
            z = None
            for j in taps:
                term = win_ref[s0 + j - r:s0 + j - r + n, :] * dwk_ref[j - first_tap:j - first_tap + 1, :]
                z = term if z is None else z + term
            zr_ref[r, s0:s0 + n, :] = z
    for r0 in range(0, tt, chunk):
        acc = zr_ref[0, r0:r0 + chunk, :]
        for r in range(1, SUBLANES):
            acc = acc + zr_ref[r, r0 + r:r0 + r + chunk, :]
        y = acc + dwb_ref[...]
        mu = jnp.mean(y, axis=-1, keepdims=True)
        yc = y - mu
        var = jnp.mean(yc * yc, axis=-1, keepdims=True)
        yn = yc * lax.rsqrt(var + EPS) * lng_ref[...] + lnb_ref[...]
        z_ref[r0:r0 + chunk, :] = yn * _sigmoid(yn)
    o_ref[0] = h_ref[0] + jnp.dot(z_ref[...].astype(BF16), wout_ref[...], preferred_element_type=F32)


def _conv_module(u, past, h, dw_k, dw_b, ln_g, ln_b, w_out_bf16):
    b, t, _ = u.shape
    tt = min(256, t)
    chunk = min(32, tt)
    nt = t // tt
    full = jnp.concatenate([jnp.zeros((b, CONV_HALO - (CONV_WIDTH - 1), D_MODEL), F32), past, u], axis=1)
    starts = np.arange(nt) * tt
    prev = jnp.stack([full[:, s:s + CONV_HALO] for s in starts], axis=1)
    vec = lambda a: a.reshape(1, D_MODEL)
    return pl.pallas_call(
        functools.partial(_conv_kernel, tt=tt, chunk=chunk),
        grid=(b, nt),
        in_specs=[pl.BlockSpec((1, 1, CONV_HALO, D_MODEL), lambda i, j: (i, j, 0, 0)),
                  pl.BlockSpec((1, tt, D_MODEL), lambda i, j: (i, j, 0)),
                  pl.BlockSpec((1, tt, D_MODEL), lambda i, j: (i, j, 0)),
                  pl.BlockSpec((CONV_WIDTH, D_MODEL), lambda i, j: (0, 0)),
                  pl.BlockSpec((1, D_MODEL), lambda i, j: (0, 0)),
                  pl.BlockSpec((1, D_MODEL), lambda i, j: (0, 0)),
                  pl.BlockSpec((1, D_MODEL), lambda i, j: (0, 0)),
                  pl.BlockSpec((D_MODEL, D_MODEL), lambda i, j: (0, 0))],
        out_specs=pl.BlockSpec((1, tt, D_MODEL), lambda i, j: (i, j, 0)),
        out_shape=jax.ShapeDtypeStruct((b, t, D_MODEL), F32),
        scratch_shapes=[pltpu.VMEM((CONV_HALO + tt + CONV_PAD, D_MODEL), F32),
                        pltpu.VMEM((SUBLANES, tt + SUBLANES, D_MODEL), F32),
                        pltpu.VMEM((tt, D_MODEL), F32)],
        compiler_params=_params("parallel", "parallel"),
        name="conformer_conv",
    )(prev, u, h, dw_k, vec(dw_b), vec(ln_g), vec(ln_b), w_out_bf16)


BF16_ROWS = 16


def _batcher_pairs(lo, hi):
    def merge(lo, hi, r):
        step = r * 2
        if step < hi - lo:
            yield from merge(lo, hi, step)
            yield from merge(lo + r, hi, step)
            yield from ((i, i + r) for i in range(lo + r, hi - r, step))
        else:
            yield (lo, lo + r)

    if hi - lo >= 1:
        mid = lo + (hi - lo) // 2
        yield from _batcher_pairs(lo, mid)
        yield from _batcher_pairs(mid + 1, hi)
        yield from merge(lo, hi, 1)


def _compare_exchange(x, i, j):
    x[i], x[j] = jnp.maximum(x[i], x[j]), jnp.minimum(x[i], x[j])


def _merge_sublanes(x):
    n = len(x)
    for shift in (4, 2, 1):
        y = [pltpu.roll(v, shift, 0) for v in x]
        x = [jnp.maximum(x[i], y[n - 1 - i]) for i in range(n)]
        d = n // 2
        while d:
            for i in range(n):
                if not i & d:
                    _compare_exchange(x, i, i + d)
            d //= 2
    return x


def _top16_sorted(s):
    x = [s[SUBLANES * j:SUBLANES * (j + 1)] for j in range(s.shape[0] // SUBLANES)]
    for i, j in _batcher_pairs(0, len(x) - 1):
        _compare_exchange(x, i, j)
    return _merge_sublanes(x)


def _bf16_bits(x):
    return pltpu.bitcast(x.astype(BF16).astype(F32), jnp.uint32)


def _bf16_pair(x):
    hi = _bf16_bits(x)
    return hi | (hi >> 16)


def _bf16_pack_rows(x):
    r, c = x.shape
    x4 = x.reshape(r // 16, 2, 8, c)
    words = _bf16_bits(x4[:, 1]) | (_bf16_bits(x4[:, 0]) >> 16)
    return words.reshape(r // 2, c)


def _bf16_unpack_rows(words):
    lo = pltpu.bitcast(words << 16, F32)
    hi = pltpu.bitcast(words & jnp.uint32(0xFFFF0000), F32)
    return jnp.stack([lo, hi], axis=1)


def _peer_topk_kernel(q_ref, k1_ref, k2_ref, n1_ref, e1_ref, r2_ref, e2_ref):
    half = PEER_DKEY // 2
    k = PEER_TOPK
    q = q_ref[...]
    s1 = lax.dot_general(k1_ref[...], q[:, :half], _NT, preferred_element_type=F32)
    s2 = lax.dot_general(k2_ref[...], q[:, half:], _NT, preferred_element_type=F32)
    v1 = _top16_sorted(s1)
    v2 = _top16_sorted(s2)
    sub = lax.broadcasted_iota(jnp.int32, v1[0].shape, 0)
    a_lo, a_hi = v1[0], v1[SUBLANES]
    n_b = jnp.full(sub.shape, k, jnp.int32)
    for r in range(1, SUBLANES):
        a_lo = jnp.where(sub == r, v1[r], a_lo)
        a_hi = jnp.where(sub == r, v1[SUBLANES + r], a_hi)
        n_b = jnp.where(sub == r, k // (r + 1), n_b)
    lists = [jnp.where(n_b > b, a_lo + v2[b], -jnp.inf) for b in range(k)]
    single = a_hi + v2[0]
    merged, x = [], single
    for b in range(k):
        merged.append(jnp.maximum(lists[b], x))
        x = jnp.minimum(lists[b], x)
    best = _merge_sublanes(merged)
    cmax, tau = best[0], best[-1]
    z = functools.reduce(jnp.add, [jnp.exp(t - cmax) for t in best])
    count_lo = functools.reduce(jnp.add, [jnp.where(t >= tau, 1.0, 0.0) for t in lists])
    count_hi = jnp.where(single >= tau, 1.0, 0.0)
    n1 = jnp.zeros(s1.shape, F32)
    for a in reversed(range(k)):
        r = a % SUBLANES
        n_a = (count_lo if a < SUBLANES else count_hi)[r:r + 1]
        n1 = jnp.where(s1 == v1[a][0:1], n_a, n1)
    rank2 = functools.reduce(jnp.add, [jnp.where(v2[b][0:1] > s2, 1.0, 0.0) for b in range(k)])
    n1_ref[0] = _bf16_pair(n1)
    e1_ref[0] = _bf16_pair(0.5 * jnp.exp(s1 - v1[0][0:1]) / z[0:1])
    r2_ref[0] = _bf16_pack_rows(rank2)
    e2_ref[0] = _bf16_pack_rows(jnp.exp(s2 - v2[0][0:1]))


def _peer_topk(q, sub_k):
    n = q.shape[0]
    tt = 256
    assert n % tt == 0
    row = pl.BlockSpec((1, PEER_KEYS, tt), lambda i, h: (h, 0, i))
    tile = pl.BlockSpec((1, PEER_KEYS // 2, tt), lambda i, h: (h, 0, i))
    row_shape = jax.ShapeDtypeStruct((PEER_HEADS, PEER_KEYS, n), jnp.uint32)
    tile_shape = jax.ShapeDtypeStruct((PEER_HEADS, PEER_KEYS // 2, n), jnp.uint32)
    return pl.pallas_call(
        _peer_topk_kernel,
        grid=(n // tt, PEER_HEADS),
        in_specs=[pl.BlockSpec((tt, PEER_DKEY), lambda i, h: (i, h)),
                  pl.BlockSpec((PEER_KEYS, PEER_DKEY // 2), lambda i, h: (0, 0)),
                  pl.BlockSpec((PEER_KEYS, PEER_DKEY // 2), lambda i, h: (0, 0))],
        out_specs=[row, row, tile, tile],
        out_shape=[row_shape, row_shape, tile_shape, tile_shape],
        compiler_params=_params("parallel", "parallel"),
        name="peer_topk",
    )(q, sub_k[0], sub_k[1])


PEER_HALF_I1 = 4
PEER_HALF = PEER_HALF_I1 * PEER_KEYS


def _peer_dense_kernel(xn_ref, u_ref, vta_ref, vtb_ref, n1_ref, e1_ref, r2_ref, e2_ref, h_ref, gout_ref,
                       o_ref, a_ref, wa_ref, wb_ref, acc_ref, *, tt, final_norm):
    e = pl.program_id(1)
    last = pl.num_programs(1) - 1

    @pl.when(e == 0)
    def _():
        acc_ref[...] = jnp.zeros_like(acc_ref)
        wb_ref[...] = jnp.zeros_like(wb_ref)

    live = jnp.where(e < last, jnp.uint32(1), jnp.uint32(0))
    n_tiles = PEER_KEYS // BF16_ROWS

    def packed_row(ref, hd, r, lanes, scale=None):
        word = ref[hd, r:r + 1, lanes]
        if scale is not None:
            word = word * scale
        return pltpu.bitcast(jnp.broadcast_to(word, (8, 128)), BF16)[None]

    def as_bf16_tiles(words):
        return pltpu.bitcast(words.reshape(n_tiles, 8, 128), BF16)

    def build(half, w_ref):
        rows_u = slice(half * PEER_HALF, (half + 1) * PEER_HALF)
        a_ref[...] = lax.dot_general(u_ref[rows_u, :], xn_ref[...], _NT, preferred_element_type=F32)
        pair = 2
        for lg in range(tt // 128):
            lanes = slice(lg * 128, (lg + 1) * 128)
            for il0 in range(0, PEER_HALF_I1, pair):
                gates = [jnp.zeros((n_tiles, BF16_ROWS, 128), BF16) for _ in range(pair)]
                for hd in range(PEER_HEADS):
                    r2 = as_bf16_tiles(r2_ref[hd, :, lanes])
                    e2 = as_bf16_tiles(e2_ref[hd, :, lanes])
                    for k in range(pair):
                        r = half * PEER_HALF_I1 + il0 + k
                        n1b = packed_row(n1_ref, hd, r, lanes, live)
                        e1b = packed_row(e1_ref, hd, r, lanes)
                        gates[k] = gates[k] + jnp.where(r2 < n1b, e2 * e1b, jnp.zeros_like(e2))
                for k in range(pair):
                    rows = slice((il0 + k) * PEER_KEYS, (il0 + k + 1) * PEER_KEYS)
                    gate = _bf16_unpack_rows(pltpu.bitcast(gates[k], jnp.uint32))
                    act = _gelu_tanh_doubled(a_ref[rows, lanes]).reshape(n_tiles, 2, 8, 128)
                    w_ref[rows, lanes] = (act * gate).reshape(PEER_KEYS, 128).astype(BF16)

    acc_ref[...] += jnp.dot(vtb_ref[...], wb_ref[...], preferred_element_type=F32)
    build(0, wa_ref)
    acc_ref[...] += jnp.dot(vta_ref[...], wa_ref[...], preferred_element_type=F32)
    build(1, wb_ref)

    @pl.when(e == last)
    def _():
        out = h_ref[...] + acc_ref[...].T
        if final_norm:
            out = _rms(out, gout_ref[...])
        o_ref[...] = out


def _peer_dense(xn_bf16, u_bf16, vt_bf16, n1, e1, r2, e2, h, g_out, final_norm):
    n = h.shape[0]
    tt = min(512, n)
    n_exp = u_bf16.shape[0]
    n_e = n_exp // (2 * PEER_HALF)
    tok = lambda i, e: (i, 0)
    cur = lambda e: jnp.minimum(e, n_e - 1)
    return pl.pallas_call(
        functools.partial(_peer_dense_kernel, tt=tt, final_norm=final_norm),
        grid=(n // tt, n_e + 1),
        in_specs=[pl.BlockSpec((tt, D_MODEL), tok),
                  pl.BlockSpec((2 * PEER_HALF, D_MODEL), lambda i, e: (cur(e), 0)),
                  pl.BlockSpec((D_MODEL, PEER_HALF), lambda i, e: (0, 2 * cur(e))),
                  pl.BlockSpec((D_MODEL, PEER_HALF), lambda i, e: (0, jnp.maximum(2 * e - 1, 0))),
                  pl.BlockSpec((PEER_HEADS, 2 * PEER_HALF_I1, tt), lambda i, e: (0, cur(e), i)),
                  pl.BlockSpec((PEER_HEADS, 2 * PEER_HALF_I1, tt), lambda i, e: (0, cur(e), i)),
                  pl.BlockSpec((PEER_HEADS, PEER_KEYS // 2, tt), lambda i, e: (0, 0, i)),
                  pl.BlockSpec((PEER_HEADS, PEER_KEYS // 2, tt), lambda i, e: (0, 0, i)),
                  pl.BlockSpec((tt, D_MODEL), tok),
                  pl.BlockSpec((1, D_MODEL), lambda i, e: (0, 0))],
        out_specs=pl.BlockSpec((tt, D_MODEL), tok),
        out_shape=jax.ShapeDtypeStruct((n, D_MODEL), F32),
        scratch_shapes=[pltpu.VMEM((PEER_HALF, tt), F32), pltpu.VMEM((PEER_HALF, tt), BF16),
                        pltpu.VMEM((PEER_HALF, tt), BF16), pltpu.VMEM((D_MODEL, tt), F32)],
        compiler_params=_params("parallel", "arbitrary"),
        name="peer_dense",
    )(xn_bf16, u_bf16, vt_bf16, vt_bf16, n1, e1, r2, e2, h, g_out.reshape(1, D_MODEL))


def _transpose_kernel(v_ref, o_ref):
    o_ref[...] = v_ref[...].T.astype(o_ref.dtype)


def _transpose_bf16(v):
    r, c = v.shape
    tr = 512
    return pl.pallas_call(
        _transpose_kernel,
        grid=(r // tr,),
        in_specs=[pl.BlockSpec((tr, c), lambda i: (i, 0))],
        out_specs=pl.BlockSpec((c, tr), lambda i: (0, i)),
        out_shape=jax.ShapeDtypeStruct((c, r), BF16),
        compiler_params=_params("parallel"),
        name="transpose_bf16",
    )(v)


def _peer_layer(h, g_norm, w_q, sub_k, u_emb, v_emb, g_out, final_norm):
    q, xn = _norm_matmul(h, g_norm, w_q.astype(BF16), "peer")
    n1, e1, r2, e2 = _peer_topk(q, sub_k)
    return _peer_dense(xn, u_emb.astype(BF16), _transpose_bf16(v_emb), n1, e1, r2, e2, h, g_out, final_norm)


GATHER_PAGES = 8
GATHER_ROWS = GATHER_PAGES * PAGE_SIZE


def _gather_kernel(pt_ref, *refs, n_steps):
    page_refs, new_ref, o_ref = refs[:GATHER_PAGES], refs[GATHER_PAGES], refs[GATHER_PAGES + 1]
    p = pl.program_id(1)

    @pl.when(p < n_steps)
    def _():
        for k, page_ref in enumerate(page_refs):
            o_ref[0, k * PAGE_SIZE:(k + 1) * PAGE_SIZE, :] = page_ref[0].T.astype(o_ref.dtype)

    @pl.when(p >= n_steps)
    def _():
        o_ref[0] = new_ref[0].astype(o_ref.dtype)


def _gather_pages(cache, page_table, new_rows, out_dtype):
    b, n_pages = page_table.shape
    assert n_pages % GATHER_PAGES == 0 and new_rows.shape[1] <= GATHER_ROWS
    n_steps = n_pages // GATHER_PAGES
    new_pad = jnp.pad(new_rows, ((0, 0), (0, GATHER_ROWS - new_rows.shape[1]), (0, 0)))

    def page_spec(k):
        return pl.BlockSpec((1, KV_COLS, PAGE_SIZE),
                            lambda i, p, pt: (pt[i, jnp.minimum(p, n_steps - 1) * GATHER_PAGES + k], 0, 0))

    grid_spec = pltpu.PrefetchScalarGridSpec(
        num_scalar_prefetch=1,
        grid=(b, n_steps + 1),
        in_specs=[page_spec(k) for k in range(GATHER_PAGES)]
        + [pl.BlockSpec((1, GATHER_ROWS, KV_COLS), lambda i, p, pt: (i, 0, 0))],
        out_specs=pl.BlockSpec((1, GATHER_ROWS, KV_COLS), lambda i, p, pt: (i, p, 0)),
    )
    return pl.pallas_call(
        functools.partial(_gather_kernel, n_steps=n_steps),
        grid_spec=grid_spec,
        out_shape=jax.ShapeDtypeStruct((b, (n_steps + 1) * GATHER_ROWS, KV_COLS), out_dtype),
        compiler_params=_params("parallel", "arbitrary"),
        name="gather_pages",
    )(page_table, *([cache] * GATHER_PAGES), new_pad)


CMP_TILE = 128


def _compress_kernel(main_ref, halo_ref, pe_ref, w1k_ref, w1v_ref, w2k_ref, w2v_ref, o_ref, win_ref):
    n_tok = CMP_TILE * CMP_STRIDE
    n_lane_tiles = KV_COLS // 128
    for c in range(n_lane_tiles):
        win_ref[c, 0:n_tok, :] = main_ref[0, :, c * 128:(c + 1) * 128]
        win_ref[c, n_tok:n_tok + CMP_STRIDE, :] = halo_ref[0, :, c * 128:(c + 1) * 128]
    hk = jnp.zeros((CMP_TILE, N_KV_HEADS * CMP_HIDDEN), F32)
    hv = jnp.zeros((CMP_TILE, N_KV_HEADS * CMP_HIDDEN), F32)
    for s in range(CMP_BLOCK):
        x = jnp.concatenate([win_ref[c, pl.ds(s, CMP_TILE, stride=CMP_STRIDE), :] for c in range(n_lane_tiles)],
                            axis=1)
        x = (x + pe_ref[s]).astype(BF16)
        hk = hk + jnp.dot(x[:, :K_COLS], w1k_ref[s], preferred_element_type=F32)
        hv = hv + jnp.dot(x[:, K_COLS:], w1v_ref[s], preferred_element_type=F32)
    o_ref[0, :, :K_COLS] = jnp.dot(_gelu_tanh(hk).astype(BF16), w2k_ref[...], preferred_element_type=F32)
    o_ref[0, :, K_COLS:] = jnp.dot(_gelu_tanh(hv).astype(BF16), w2v_ref[...], preferred_element_type=F32)


def _block_diag(w):
    eye = jnp.eye(N_KV_HEADS, dtype=w.dtype)
    out = jnp.einsum("gh,...ab->...gahb", eye, w)
    return out.reshape(w.shape[:-2] + (N_KV_HEADS * w.shape[-2], N_KV_HEADS * w.shape[-1]))


def _compress(rows, n_tiles, cmp_pe, cmp_w1, cmp_w2):
    b, length = rows.shape[:2]
    n_tok = CMP_TILE * CMP_STRIDE
    last_halo = length // CMP_STRIDE - 1
    w1 = cmp_w1.reshape(2, CMP_BLOCK, HEAD_DIM, CMP_HIDDEN)
    w1k, w1v = _block_diag(w1[0]).astype(BF16), _block_diag(w1[1]).astype(BF16)
    w2k, w2v = _block_diag(cmp_w2[0]).astype(BF16), _block_diag(cmp_w2[1]).astype(BF16)
    pe = jnp.concatenate([jnp.tile(cmp_pe[0], (1, N_KV_HEADS)), jnp.tile(cmp_pe[1], (1, N_KV_HEADS))], axis=1)
    pe = pe.reshape(CMP_BLOCK, 1, KV_COLS)
    const3 = lambda i, j: (0, 0, 0)
    return pl.pallas_call(
        _compress_kernel,
        grid=(b, n_tiles),
        in_specs=[pl.BlockSpec((1, n_tok, KV_COLS), lambda i, j: (i, j, 0)),
                  pl.BlockSpec((1, CMP_STRIDE, KV_COLS),
                               lambda i, j: (i, jnp.minimum((j + 1) * CMP_TILE, last_halo), 0)),
                  pl.BlockSpec((CMP_BLOCK, 1, KV_COLS), const3),
                  pl.BlockSpec((CMP_BLOCK, K_COLS, N_KV_HEADS * CMP_HIDDEN), const3),
                  pl.BlockSpec((CMP_BLOCK, K_COLS, N_KV_HEADS * CMP_HIDDEN), const3),
                  pl.BlockSpec((N_KV_HEADS * CMP_HIDDEN, K_COLS), lambda i, j: (0, 0)),
                  pl.BlockSpec((N_KV_HEADS * CMP_HIDDEN, K_COLS), lambda i, j: (0, 0))],
        out_specs=pl.BlockSpec((1, CMP_TILE, KV_COLS), lambda i, j: (i, j, 0)),
        out_shape=jax.ShapeDtypeStruct((b, n_tiles * CMP_TILE, KV_COLS), F32),
        scratch_shapes=[pltpu.VMEM((KV_COLS // 128, n_tok + CMP_STRIDE, 128), F32)],
        compiler_params=_params("parallel", "parallel"),
        name="cmp_compress",
    )(rows, rows, pe, w1k, w1v, w2k, w2v)


def _head_perm():
    c = np.arange(D_MODEL)
    h, g, d = c // K_COLS, (c % K_COLS) // HEAD_DIM, c % HEAD_DIM
    return (g * HEADS_PER_KV + h) * HEAD_DIM + d


def _gate_expand(branch):
    c = np.arange(D_MODEL)
    h, g = c // K_COLS, (c % K_COLS) // HEAD_DIM
    m = np.zeros((GATE_COLS, D_MODEL), np.float32)
    m[(g * HEADS_PER_KV + h) * N_BRANCH + branch, c] = 1.0
    return jnp.asarray(m)


def _block_diag_queries(q):
    tq = q.shape[0]
    lane_g = lax.broadcasted_iota(jnp.int32, (tq, K_COLS), 1) // HEAD_DIM
    blocks = []
    for g in range(N_KV_HEADS):
        for h in range(HEADS_PER_KV):
            blocks.append(jnp.where(lane_g == g, q[:, h * K_COLS:(h + 1) * K_COLS], 0.0))
    return jnp.concatenate(blocks, axis=0)


def _diag_heads(acc, tq):
    lane_g = lax.broadcasted_iota(jnp.int32, (tq, K_COLS), 1) // HEAD_DIM
    outs = []
    for h in range(HEADS_PER_KV):
        o = jnp.zeros((tq, K_COLS), F32)
        for g in range(N_KV_HEADS):
            r0 = (g * HEADS_PER_KV + h) * tq
            o = o + jnp.where(lane_g == g, acc[r0:r0 + tq], 0.0)
        outs.append(o)
    return jnp.concatenate(outs, axis=1)


_NT = (((1,), (1,)), ((), ()))


def _cmp_attn_kernel(q_ref, gate_ref, kvc_ref, ovt_ref, eg_ref, oc_ref, sel_ref, sc_ref,
                     *, tq, pos0, n_slc_loop):
    q0 = pos0 + pl.program_id(1) * tq
    n_pad = kvc_ref.shape[1]
    s_pad = ovt_ref.shape[0]
    rows = N_HEADS * tq
    qbd = _block_diag_queries(q_ref[0]).astype(BF16)
    kc = kvc_ref[0, :, :K_COLS].astype(BF16)
    vc = kvc_ref[0, :, K_COLS:].astype(BF16)
    s = lax.dot_general(qbd, kc, _NT, preferred_element_type=F32)
    t_row = q0 + (lax.broadcasted_iota(jnp.int32, (rows, n_pad), 0) & (tq - 1))
    n_col = lax.broadcasted_iota(jnp.int32, (rows, n_pad), 1)
    vis = (n_col * CMP_STRIDE + (CMP_BLOCK - 1)) <= t_row
    m = jnp.max(jnp.where(vis, s, -jnp.inf), axis=-1, keepdims=True)
    m = jnp.where(m > -jnp.inf, m, 0.0)
    e = jnp.where(vis, jnp.exp(s - m), 0.0)
    p = e / jnp.maximum(jnp.sum(e, axis=-1, keepdims=True), 1e-30)
    oc = _diag_heads(jnp.dot(p.astype(BF16), vc, preferred_element_type=F32), tq)
    oc_ref[0] = oc * jnp.dot(gate_ref[0], eg_ref[...], preferred_element_type=F32)

    psum = jnp.concatenate(
        [sum(p[(g * HEADS_PER_KV + h) * tq:(g * HEADS_PER_KV + h + 1) * tq] for h in range(HEADS_PER_KV))
         for g in range(N_KV_HEADS)], axis=0)
    imp_t = lax.dot_general(ovt_ref[...], psum, _NT, preferred_element_type=F32)
    cols = N_KV_HEADS * tq
    blk = lax.broadcasted_iota(jnp.int32, (s_pad, cols), 0)
    cur = (q0 + (lax.broadcasted_iota(jnp.int32, (s_pad, cols), 1) & (tq - 1))) // SLC_BLOCK
    forced = (blk == 0) | (blk == cur) | (blk == cur - 1)
    score = jnp.where(forced, jnp.inf, jnp.where(blk <= cur, imp_t, -jnp.inf))
    sc_ref[...] = score

    score = score[:n_slc_loop]
    blk = lax.broadcasted_iota(jnp.int32, (n_slc_loop, cols), 0)
    cur = (q0 + (lax.broadcasted_iota(jnp.int32, (n_slc_loop, cols), 1) & (tq - 1))) // SLC_BLOCK

    def rank_body(jb, rank):
        rows8 = sc_ref[pl.ds(pl.multiple_of(jb * 8, 8), 8), :]
        for r in range(8):
            row = rows8[r:r + 1]
            tie = jnp.where(row == score, jnp.where(jb * 8 + r < blk, 1.0, 0.0), 0.0)
            rank = rank + jnp.where(row > score, 1.0, tie)
        return rank

    rank = lax.fori_loop(0, n_slc_loop // 8, rank_body, jnp.zeros((n_slc_loop, cols), F32))
    sel_t = jnp.where(rank < float(N_SELECT), jnp.where(blk <= cur, 1.0, 0.0), 0.0)
    if n_slc_loop < s_pad:
        sel_t = jnp.concatenate([sel_t, jnp.zeros((s_pad - n_slc_loop, cols), F32)], axis=0)
    sel_t = sel_t.astype(BF16)
    ci = lax.broadcasted_iota(jnp.int32, (cols, cols), 0)
    cj = lax.broadcasted_iota(jnp.int32, (cols, cols), 1)
    eye = jnp.where(ci == cj, 1.0, 0.0).astype(BF16)
    sel_ref[0, 0] = lax.dot_general(eye, sel_t, _NT, preferred_element_type=F32)


def _cmp_attn(q, gates, kvc, ov_t, pos0, tq, n_slc):
    b, t, _ = q.shape
    s_pad = ov_t.shape[0]
    n_pad = kvc.shape[1]
    n_slc_loop = -(-n_slc // 8) * 8
    return pl.pallas_call(
        functools.partial(_cmp_attn_kernel, tq=tq, pos0=pos0, n_slc_loop=n_slc_loop),
        grid=(b, t // tq),
        in_specs=[pl.BlockSpec((1, tq, D_MODEL), lambda i, j: (i, j, 0)),
                  pl.BlockSpec((1, tq, GATE_COLS), lambda i, j: (i, j, 0)),
                  pl.BlockSpec((1, n_pad, KV_COLS), lambda i, j: (i, 0, 0)),
                  pl.BlockSpec((s_pad, n_pad), lambda i, j: (0, 0)),
                  pl.BlockSpec((GATE_COLS, D_MODEL), lambda i, j: (0, 0))],
        out_specs=[pl.BlockSpec((1, tq, D_MODEL), lambda i, j: (i, j, 0)),
                   pl.BlockSpec((1, 1, N_KV_HEADS * tq, s_pad), lambda i, j: (i, j, 0, 0))],
        out_shape=[jax.ShapeDtypeStruct((b, t, D_MODEL), F32),
                   jax.ShapeDtypeStruct((b, t // tq, N_KV_HEADS * tq, s_pad), F32)],
        scratch_shapes=[pltpu.VMEM((s_pad, N_KV_HEADS * tq), F32)],
        compiler_params=_params("parallel", "parallel"),
        name="nsa_cmp_select",
    )(q, gates, kvc, ov_t, _gate_expand(0))


WIN_KEYS = 768
ATTN_ROWS = 64


def _lane_tiles(x):
    return [x[:, i * 128:(i + 1) * 128] for i in range(x.shape[1] // 128)]


def _row_max(x):
    return jnp.max(functools.reduce(jnp.maximum, _lane_tiles(x)), axis=-1, keepdims=True)


def _row_sum(x):
    return jnp.sum(functools.reduce(jnp.add, _lane_tiles(x)), axis=-1, keepdims=True)


def _sparse_attn_kernel(q_ref, gate_ref, sel_ref, oc_ref, kslc_ref, kwin_ref, egs_ref, egw_ref, o_ref,
                        qaug_ref, s_ref, p_ref, sw_ref, pw_ref, m_ref, l_ref, alpha_ref, acc_ref,
                        *, tq, tk, pos0, win_pos0):
    q0 = pos0 + pl.program_id(1) * tq
    rows = N_HEADS * tq
    s_pad = sel_ref.shape[3]
    rb = min(ATTN_ROWS, rows)

    qaug_ref[:, 0:K_COLS] = _block_diag_queries(q_ref[0]).astype(BF16)
    not_sel = sel_ref[0, 0] - 1.0
    qaug_ref[:, K_COLS:] = jnp.concatenate(
        [not_sel[g * tq:(g + 1) * tq] for g in range(N_KV_HEADS) for _ in range(HEADS_PER_KV)],
        axis=0).astype(BF16)

    def q_pos(r0, width):
        return q0 + ((r0 + lax.broadcasted_iota(jnp.int32, (rb, width), 0)) & (tq - 1))

    m_ref[...] = jnp.full_like(m_ref, NEG_BIG)
    l_ref[...] = jnp.zeros_like(l_ref)
    acc_ref[...] = jnp.zeros_like(acc_ref)
    blk_lane = lax.broadcasted_iota(jnp.int32, (tk, s_pad), 1)
    blk_key = lax.broadcasted_iota(jnp.int32, (tk, s_pad), 0) // SLC_BLOCK
    k_col = lax.broadcasted_iota(jnp.int32, (rb, tk), 1)

    n_split = s_ref.shape[0]
    part = rows // n_split

    def slc_tile(j, causal):
        ks = pl.ds(pl.multiple_of(j * tk, tk), tk)
        onehot = jnp.where(blk_lane == j * (tk // SLC_BLOCK) + blk_key, -NEG_BIG, 0.0).astype(BF16)
        kaug = jnp.concatenate([kslc_ref[0, ks, 0:K_COLS], onehot], axis=1)
        vt = kslc_ref[0, ks, K_COLS:KV_COLS]
        for sp in range(n_split):
            s_ref[sp] = lax.dot_general(qaug_ref[sp * part:(sp + 1) * part, :], kaug, _NT,
                                        preferred_element_type=F32)
        for sp in range(n_split):
            for r0 in range(0, part, rb):
                blk = slice(sp * part + r0, sp * part + r0 + rb)
                s = s_ref[sp, r0:r0 + rb, :]
                if causal:
                    s = jnp.where(j * tk + k_col <= q_pos(sp * part + r0, tk), s, NEG_BIG)
                m_old = m_ref[blk, :]
                m_new = jnp.maximum(m_old, _row_max(s))
                p = jnp.exp(s - m_new)
                alpha = jnp.exp(m_old - m_new)
                l_ref[blk, :] = alpha * l_ref[blk, :] + _row_sum(p)
                m_ref[blk, :] = m_new
                alpha_ref[blk, :] = alpha
                p_ref[sp, r0:r0 + rb, :] = p.astype(BF16)
            prows = slice(sp * part, (sp + 1) * part)
            acc_ref[prows, :] = alpha_ref[prows, :] * acc_ref[prows, :] + jnp.dot(
                p_ref[sp], vt, preferred_element_type=F32)

    last_tile = (q0 + tq - 1) // tk
    lax.fori_loop(0, last_tile, lambda j, c: (slc_tile(j, False), c)[1], 0)
    slc_tile(last_tile, True)
    o_s = _diag_heads(acc_ref[...] / jnp.maximum(l_ref[...], 1e-30), tq)

    lw = kwin_ref.shape[1]
    w0 = jnp.clip((q0 - win_pos0 - WINDOW) // 128 * 128, 0, lw - WIN_KEYS)
    wk = pl.ds(pl.multiple_of(w0, 128), WIN_KEYS)
    kw = kwin_ref[0, wk, 0:K_COLS]
    vw = kwin_ref[0, wk, K_COLS:KV_COLS]
    for sp in range(n_split):
        sw_ref[sp] = lax.dot_general(qaug_ref[sp * part:(sp + 1) * part, 0:K_COLS], kw, _NT,
                                     preferred_element_type=F32)
    kpos = win_pos0 + w0 + lax.broadcasted_iota(jnp.int32, (rb, WIN_KEYS), 1)
    for sp in range(n_split):
        for r0 in range(0, part, rb):
            blk = slice(sp * part + r0, sp * part + r0 + rb)
            t = q_pos(sp * part + r0, WIN_KEYS)
            s = jnp.where((kpos <= t) & (kpos > t - WINDOW), sw_ref[sp, r0:r0 + rb, :], NEG_BIG)
            p = jnp.exp(s - _row_max(s))
            l_ref[blk, :] = _row_sum(p)
            pw_ref[sp, r0:r0 + rb, :] = p.astype(BF16)
        prows = slice(sp * part, (sp + 1) * part)
        acc_ref[prows, :] = jnp.dot(pw_ref[sp], vw, preferred_element_type=F32)
    o_w = _diag_heads(acc_ref[...] / jnp.maximum(l_ref[...], 1e-30), tq)

    gate = gate_ref[0]
    g_s = jnp.dot(gate, egs_ref[...], preferred_element_type=F32)
    g_w = jnp.dot(gate, egw_ref[...], preferred_element_type=F32)
    o_ref[0] = (oc_ref[0] + g_s * o_s + g_w * o_w).astype(o_ref.dtype)


def _sparse_attn(q, gates, sel, oc, kslc, kwin, slc_col, win_col, pos0, win_pos0, tq, tk):
    b, t, _ = q.shape
    s_pad = sel.shape[3]
    ls, lw = kslc.shape[1], kwin.shape[1]
    assert ls % tk == 0 and lw % 128 == 0 and lw >= WIN_KEYS and tk % tq == 0
    rows = N_HEADS * tq
    n_split = max(1, min(4, rows // (2 * ATTN_ROWS)))
    qtile = lambda i, j: (i, j, 0)
    return pl.pallas_call(
        functools.partial(_sparse_attn_kernel, tq=tq, tk=tk, pos0=pos0, win_pos0=win_pos0),
        grid=(b, t // tq),
        in_specs=[pl.BlockSpec((1, tq, D_MODEL), qtile),
                  pl.BlockSpec((1, tq, GATE_COLS), qtile),
                  pl.BlockSpec((1, 1, N_KV_HEADS * tq, s_pad), lambda i, j: (i, j, 0, 0)),
                  pl.BlockSpec((1, tq, D_MODEL), qtile),
                  pl.BlockSpec((1, ls, KV_COLS), lambda i, j: (i, 0, slc_col)),
                  pl.BlockSpec((1, lw, KV_COLS), lambda i, j: (i, 0, win_col)),
                  pl.BlockSpec((GATE_COLS, D_MODEL), lambda i, j: (0, 0)),
                  pl.BlockSpec((GATE_COLS, D_MODEL), lambda i, j: (0, 0))],
        out_specs=pl.BlockSpec((1, tq, D_MODEL), qtile),
        out_shape=jax.ShapeDtypeStruct((b, t, D_MODEL), F32),
        scratch_shapes=[pltpu.VMEM((rows, K_COLS + s_pad), BF16),
                        pltpu.VMEM((n_split, rows // n_split, tk), F32),
                        pltpu.VMEM((n_split, rows // n_split, tk), BF16),
                        pltpu.VMEM((n_split, rows // n_split, WIN_KEYS), F32),
                        pltpu.VMEM((n_split, rows // n_split, WIN_KEYS), BF16),
                        pltpu.VMEM((rows, 1), F32), pltpu.VMEM((rows, 1), F32), pltpu.VMEM((rows, 1), F32),
                        pltpu.VMEM((rows, K_COLS), F32)],
        compiler_params=_params("parallel", "arbitrary"),
        name="nsa_select_window",
    )(q, gates, sel, oc, kslc, kwin, _gate_expand(1), _gate_expand(2))


ATTN_TK = 512


def _overlap_t(s_pad, n_pad, n_slc, n_cmp):
    n = np.arange(n_pad)[None, :]
    s = np.arange(s_pad)[:, None]
    ov = ((n * CMP_STRIDE < s * SLC_BLOCK + SLC_BLOCK) & (n * CMP_STRIDE + CMP_BLOCK - 1 >= s * SLC_BLOCK)
          & (n < n_cmp) & (s < n_slc))
    return jnp.asarray(ov.astype(np.float32))


def _trunk(x, conv_past, past, pos0, p):
    b, t, _ = x.shape
    n = b * t
    kvh = (2, N_KV_HEADS, HEAD_DIM)
    h = x.reshape(n, D_MODEL)

    (u,) = _norm_matmul(h, p["norm_mix_g"][0], p["conv_w_in"][0].astype(BF16), "glu")
    u = u.reshape(b, t, D_MODEL)
    h = _conv_module(u, conv_past, h.reshape(b, t, D_MODEL), p["conv_dw_k"][0], p["conv_dw_b"][0],
                     p["conv_ln_g"][0], p["conv_ln_b"][0], p["conv_w_out"][0].astype(BF16)).reshape(n, D_MODEL)
    conv_state = jnp.concatenate([conv_past, u], axis=1)[:, -(CONV_WIDTH - 1):][None]
    h = _peer_layer(h, p["norm_ffn_g"][0], p["peer_w_q"][0], p["peer_sub_k"][0], p["peer_u"][0],
                    p["peer_v"][0], p["norm_out_g"], False)

    kv, kv_bf16 = _norm_matmul(h, p["norm_kv_g"], p["nsa_w_kv"].astype(BF16), "kv")
    kv = kv.reshape(b, t, N_BRANCH * KV_COLS)
    kv_bf16 = kv_bf16.reshape(b, t, N_BRANCH * KV_COLS)
    new_cmp, new_slc, new_win = (kv[..., i * KV_COLS:(i + 1) * KV_COLS] for i in range(N_BRANCH))
    if past is None:
        length, win_pos0 = t, pos0
        cmp_rows, k_slc, k_win, slc_col, win_col = kv, kv_bf16, kv_bf16, 1, 2
        win_all = new_win
    else:
        cache_cmp, cache_slc, cache_win, page_table = past
        length, win_pos0 = pos0 + t, pos0 - cache_win.shape[1]
        cmp_rows = _gather_pages(cache_cmp, page_table, new_cmp, F32)
        k_slc = _gather_pages(cache_slc, page_table, new_slc, BF16)
        win_all = jnp.concatenate([cache_win, new_win], axis=1)
        lw = max(WIN_KEYS, -(-win_all.shape[1] // 128) * 128)
        k_win = jnp.pad(win_all, ((0, 0), (0, lw - win_all.shape[1]), (0, 0))).astype(BF16)
        slc_col, win_col = 0, 0
    n_cmp = (length - CMP_BLOCK) // CMP_STRIDE + 1
    n_slc = -(-length // SLC_BLOCK)
    n_tiles = -(-n_cmp // CMP_TILE)
    kvc = _compress(cmp_rows, n_tiles, p["nsa_cmp_pe"], p["nsa_cmp_w1"], p["nsa_cmp_w2"])

    perm = _head_perm()
    w_in = p["nsa_w_in"][0]
    w_in = jnp.concatenate([w_in[:, perm], w_in[:, D_MODEL:],
                            jnp.zeros((D_MODEL, GATE_COLS - N_HEADS * N_BRANCH), F32)], axis=1).astype(BF16)
    q, gates = _norm_matmul(h, p["norm_mix_g"][1], w_in, "nsa")
    q = q.reshape(b, t, D_MODEL)
    gates = gates.reshape(b, t, GATE_COLS)
    tq = min(64, t)
    s_pad = -(-n_slc // 128) * 128
    oc, sel = _cmp_attn(q, gates, kvc, _overlap_t(s_pad, kvc.shape[1], n_slc, n_cmp), pos0, tq, n_slc)
    o = _sparse_attn(q, gates, sel, oc, k_slc, k_win, slc_col, win_col, pos0, win_pos0, tq, ATTN_TK)
    h = _matmul_res(o.reshape(n, D_MODEL), p["nsa_w_o"][0][perm, :].astype(BF16), h)
    y = _peer_layer(h, p["norm_ffn_g"][1], p["peer_w_q"][1], p["peer_sub_k"][1], p["peer_u"][1],
                    p["peer_v"][1], p["norm_out_g"], True)
    keep = min(WINDOW, win_all.shape[1])
    shape5 = lambda a: a.reshape(a.shape[:2] + kvh)
    return (y.reshape(b, t, D_MODEL), conv_state, shape5(new_cmp), shape5(new_slc), shape5(win_all[:, -keep:]))


def kernel(x_prompt, x_sample, state_conv, cache_cmp_kv, cache_slc_kv, cache_win_kv, page_table, norm_mix_g, norm_ffn_g, norm_kv_g, norm_out_g, conv_w_in, conv_dw_k, conv_dw_b, conv_ln_g, conv_ln_b, conv_w_out, nsa_w_kv, nsa_cmp_pe, nsa_cmp_w1, nsa_cmp_w2, nsa_w_in, nsa_w_o, peer_w_q, peer_sub_k, peer_u, peer_v):
    p = dict(norm_mix_g=norm_mix_g, norm_ffn_g=norm_ffn_g, norm_kv_g=norm_kv_g, norm_out_g=norm_out_g,
             conv_w_in=conv_w_in, conv_dw_k=conv_dw_k, conv_dw_b=conv_dw_b, conv_ln_g=conv_ln_g,
             conv_ln_b=conv_ln_b, conv_w_out=conv_w_out, nsa_w_kv=nsa_w_kv, nsa_cmp_pe=nsa_cmp_pe,
             nsa_cmp_w1=nsa_cmp_w1, nsa_cmp_w2=nsa_cmp_w2, nsa_w_in=nsa_w_in, nsa_w_o=nsa_w_o,
             peer_w_q=peer_w_q, peer_sub_k=peer_sub_k, peer_u=peer_u, peer_v=peer_v)
    bsz = x_prompt.shape[0]
    conv0 = jnp.zeros((bsz, CONV_WIDTH - 1, D_MODEL), x_prompt.dtype)
    y_p, conv_p, cmp_p, slc_p, win_p = _trunk(x_prompt, conv0, None, 0, p)
    dec_b, n_pages = page_table.shape
    flat = lambda c: c.reshape(c.shape[0], c.shape[1], KV_COLS)
    paged = lambda c: c.transpose(0, 2, 3, 4, 1).reshape(c.shape[0], KV_COLS, c.shape[1])
    past = (paged(cache_cmp_kv), paged(cache_slc_kv), flat(cache_win_kv), page_table)
    y_s, conv_s, cmp_s, slc_s, win_s = _trunk(x_sample, state_conv[0], past, n_pages * PAGE_SIZE, p)
    return (y_p, y_s, conv_p, cmp_p, slc_p, win_p, conv_s, cmp_s, slc_s, win_s)
```

```python
import functools
import math

import numpy as np
import jax
import jax.numpy as jnp
from jax import lax
from jax.experimental import pallas as pl
from jax.experimental.pallas import tpu as pltpu

F32 = jnp.float32
BF16 = jnp.bfloat16

D_MODEL = 1024
EPS = 1e-6
CONV_WIDTH = 31
CONV_HALO = 32
CONV_PAD = 8
SUBLANES = 8
N_HEADS = 16
HEAD_DIM = 64
N_KV_HEADS = 4
HEADS_PER_KV = 4
N_BRANCH = 3
KV_COLS = 2 * N_KV_HEADS * HEAD_DIM
K_COLS = N_KV_HEADS * HEAD_DIM
CMP_BLOCK = 32
CMP_STRIDE = 16
CMP_HIDDEN = 128
SLC_BLOCK = 64
N_SELECT = 16
WINDOW = 512
PAGE_SIZE = 128
PEER_HEADS = 8
PEER_KEYS = 128
PEER_TOPK = 16
PEER_DKEY = 256
GATE_COLS = 128

VMEM_LIMIT_BYTES = 56 * 1024 * 1024
NEG_BIG = -1e30


def _params(*sem):
    return pltpu.CompilerParams(dimension_semantics=sem, vmem_limit_bytes=VMEM_LIMIT_BYTES)


def _gelu_tanh(x):
    return 0.5 * x * (1.0 + jnp.tanh(0.7978845608028654 * (x + 0.044715 * (x * x * x))))


def _gelu_tanh_doubled(x):
    inner = x * (0.7978845608028654 * 0.044715 * (x * x) + 0.7978845608028654)
    return x * jnp.tanh(inner) + x


def _sigmoid(x):
    return 1.0 / (1.0 + jnp.exp(-x))


def _rms(x, g):
    return x * lax.rsqrt(jnp.mean(x * x, axis=-1, keepdims=True) + EPS) * g


def _norm_matmul_kernel(x_ref, g_ref, w_ref, *out_refs, mode):
    xn = _rms(x_ref[...], g_ref[...])
    xb = xn.astype(BF16)
    y = jnp.dot(xb, w_ref[...], preferred_element_type=F32)
    if mode == "glu":
        out_refs[0][...] = y[:, :D_MODEL] * _sigmoid(y[:, D_MODEL:])
    elif mode == "peer":
        out_refs[0][...] = y
        out_refs[1][...] = xb
    elif mode == "kv":
        out_refs[0][...] = y
        out_refs[1][...] = y.astype(BF16)
    elif mode == "nsa":
        out_refs[0][...] = y[:, :D_MODEL] * (HEAD_DIM ** -0.5)
        out_refs[1][...] = _sigmoid(y[:, D_MODEL:])
    else:
        raise ValueError(mode)


def _norm_matmul(x, g, w_bf16, mode):
    n = x.shape[0]
    tm = min(512, n)
    assert n % tm == 0
    ncol = w_bf16.shape[1]
    if mode == "glu":
        outs = [(D_MODEL, F32)]
    elif mode == "peer":
        outs = [(ncol, F32), (D_MODEL, BF16)]
    elif mode == "kv":
        outs = [(ncol, F32), (ncol, BF16)]
    elif mode == "nsa":
        outs = [(D_MODEL, F32), (GATE_COLS, F32)]
    res = pl.pallas_call(
        functools.partial(_norm_matmul_kernel, mode=mode),
        grid=(n // tm,),
        in_specs=[pl.BlockSpec((tm, D_MODEL), lambda i: (i, 0)),
                  pl.BlockSpec((1, D_MODEL), lambda i: (0, 0)),
                  pl.BlockSpec((D_MODEL, ncol), lambda i: (0, 0))],
        out_specs=[pl.BlockSpec((tm, c), lambda i: (i, 0)) for c, _ in outs],
        out_shape=[jax.ShapeDtypeStruct((n, c), dt) for c, dt in outs],
        compiler_params=_params("parallel"),
        name="norm_matmul_" + mode,
    )(x, g.reshape(1, D_MODEL), w_bf16)
    return res


def _matmul_res_kernel(a_ref, w_ref, h_ref, o_ref):
    o_ref[...] = h_ref[...] + jnp.dot(a_ref[...].astype(BF16), w_ref[...], preferred_element_type=F32)


def _matmul_res(a, w_bf16, h):
    n = h.shape[0]
    tm = min(512, n)
    return pl.pallas_call(
        _matmul_res_kernel,
        grid=(n // tm,),
        in_specs=[pl.BlockSpec((tm, D_MODEL), lambda i: (i, 0)),
                  pl.BlockSpec((D_MODEL, D_MODEL), lambda i: (0, 0)),
                  pl.BlockSpec((tm, D_MODEL), lambda i: (i, 0))],
        out_specs=pl.BlockSpec((tm, D_MODEL), lambda i: (i, 0)),
        out_shape=jax.ShapeDtypeStruct((n, D_MODEL), F32),
        compiler_params=_params("parallel"),
        name="attn_out_proj",
    )(a, w_bf16, h)


def _conv_kernel(prev_ref, u_ref, h_ref, dwk_ref, dwb_ref, lng_ref, lnb_ref, wout_ref, o_ref,
                 win_ref, zr_ref, z_ref, *, tt, chunk):
    win_ref[0:CONV_HALO, :] = prev_ref[0, 0]
    win_ref[CONV_HALO:CONV_HALO + tt, :] = u_ref[0]
    first_tap = CONV_HALO - (CONV_WIDTH - 1)

    win_ref[CONV_HALO + tt:, :] = jnp.zeros((CONV_PAD, D_MODEL), F32)
    n_z = tt + SUBLANES
    for r in range(SUBLANES):
        taps = [j for j in range(r, first_tap + CONV_WIDTH, SUBLANES) if j >= first_tap]
        for s0 in range(0, n_z, chunk):
            n = min(chunk, n_z - s0)
            z = None
            for j in taps:
                term = win_ref[s0 + j - r:s0 + j - r + n, :] * dwk_ref[j - first_tap:j - first_tap + 1, :]
                z = term if z is None else z + term
            zr_ref[r, s0:s0 + n, :] = z
    for r0 in range(0, tt, chunk):
        acc = zr_ref[0, r0:r0 + chunk, :]
        for r in range(1, SUBLANES):
            acc = acc + zr_ref[r, r0 + r:r0 + r + chunk, :]
        y = acc + dwb_ref[...]
        mu = jnp.mean(y, axis=-1, keepdims=True)
        yc = y - mu
        var = jnp.mean(yc * yc, axis=-1, keepdims=True)
        yn = yc * lax.rsqrt(var + EPS) * lng_ref[...] + lnb_ref[...]
        z_ref[r0:r0 + chunk, :] = yn * _sigmoid(yn)
    o_ref[0] = h_ref[0] + jnp.dot(z_ref[...].astype(BF16), wout_ref[...], preferred_element_type=F32)


def _conv_module(u, past, h, dw_k, dw_b, ln_g, ln_b, w_out_bf16):
    b, t, _ = u.shape
    tt = min(256, t)
    chunk = min(32, tt)
    nt = t // tt
    full = jnp.concatenate([jnp.zeros((b, CONV_HALO - (CONV_WIDTH - 1), D_MODEL), F32), past, u], axis=1)
    starts = np.arange(nt) * tt
    prev = jnp.stack([full[:, s:s + CONV_HALO] for s in starts], axis=1)
    vec = lambda a: a.reshape(1, D_MODEL)
    return pl.pallas_call(
        functools.partial(_conv_kernel, tt=tt, chunk=chunk),
        grid=(b, nt),
        in_specs=[pl.BlockSpec((1, 1, CONV_HALO, D_MODEL), lambda i, j: (i, j, 0, 0)),
                  pl.BlockSpec((1, tt, D_MODEL), lambda i, j: (i, j, 0)),
                  pl.BlockSpec((1, tt, D_MODEL), lambda i, j: (i, j, 0)),
                  pl.BlockSpec((CONV_WIDTH, D_MODEL), lambda i, j: (0, 0)),
                  pl.BlockSpec((1, D_MODEL), lambda i, j: (0, 0)),
                  pl.BlockSpec((1, D_MODEL), lambda i, j: (0, 0)),
                  pl.BlockSpec((1, D_MODEL), lambda i, j: (0, 0)),
                  pl.BlockSpec((D_MODEL, D_MODEL), lambda i, j: (0, 0))],
        out_specs=pl.BlockSpec((1, tt, D_MODEL), lambda i, j: (i, j, 0)),
        out_shape=jax.ShapeDtypeStruct((b, t, D_MODEL), F32),
        scratch_shapes=[pltpu.VMEM((CONV_HALO + tt + CONV_PAD, D_MODEL), F32),
                        pltpu.VMEM((SUBLANES, tt + SUBLANES, D_MODEL), F32),
                        pltpu.VMEM((tt, D_MODEL), F32)],
        compiler_params=_params("parallel", "parallel"),
        name="conformer_conv",
    )(prev, u, h, dw_k, vec(dw_b), vec(ln_g), vec(ln_b), w_out_bf16)


BF16_ROWS = 16


def _batcher_pairs(lo, hi):
    def merge(lo, hi, r):
        step = r * 2
        if step < hi - lo:
            yield from merge(lo, hi, step)
            yield from merge(lo + r, hi, step)
            yield from ((i, i + r) for i in range(lo + r, hi - r, step))
        else:
            yield (lo, lo + r)

    if hi - lo >= 1:
        mid = lo + (hi - lo) // 2
        yield from _batcher_pairs(lo, mid)
        yield from _batcher_pairs(mid + 1, hi)
        yield from merge(lo, hi, 1)


def _compare_exchange(x, i, j):
    x[i], x[j] = jnp.maximum(x[i], x[j]), jnp.minimum(x[i], x[j])


def _merge_sublanes(x):
    n = len(x)
    for shift in (4, 2, 1):
        y = [pltpu.roll(v, shift, 0) for v in x]
        x = [jnp.maximum(x[i], y[n - 1 - i]) for i in range(n)]
        d = n // 2
        while d:
            for i in range(n):
                if not i & d:
                    _compare_exchange(x, i, i + d)
            d //= 2
    return x


def _top16_sorted(s):
    x = [s[SUBLANES * j:SUBLANES * (j + 1)] for j in range(s.shape[0] // SUBLANES)]
    for i, j in _batcher_pairs(0, len(x) - 1):
        _compare_exchange(x, i, j)
    return _merge_sublanes(x)


def _bf16_bits(x):
    return pltpu.bitcast(x.astype(BF16).astype(F32), jnp.uint32)


def _bf16_pair(x):
    hi = _bf16_bits(x)
    return hi | (hi >> 16)


def _bf16_pack_rows(x):
    r, c = x.shape
    x4 = x.reshape(r // 16, 2, 8, c)
    words = _bf16_bits(x4[:, 1]) | (_bf16_bits(x4[:, 0]) >> 16)
    return words.reshape(r // 2, c)


def _bf16_unpack_rows(words):
    lo = pltpu.bitcast(words << 16, F32)
    hi = pltpu.bitcast(words & jnp.uint32(0xFFFF0000), F32)
    return jnp.stack([lo, hi], axis=1)


def _peer_topk_kernel(q_ref, k1_ref, k2_ref, n1_ref, e1_ref, r2_ref, e2_ref):
    half = PEER_DKEY // 2
    k = PEER_TOPK
    q = q_ref[...]
    s1 = lax.dot_general(k1_ref[...], q[:, :half], _NT, preferred_element_type=F32)
    s2 = lax.dot_general(k2_ref[...], q[:, half:], _NT, preferred_element_type=F32)
    v1 = _top16_sorted(s1)
    v2 = _top16_sorted(s2)
    sub = lax.broadcasted_iota(jnp.int32, v1[0].shape, 0)
    a_lo, a_hi = v1[0], v1[SUBLANES]
    n_b = jnp.full(sub.shape, k, jnp.int32)
    for r in range(1, SUBLANES):
        a_lo = jnp.where(sub == r, v1[r], a_lo)
        a_hi = jnp.where(sub == r, v1[SUBLANES + r], a_hi)
        n_b = jnp.where(sub == r, k // (r + 1), n_b)
    lists = [jnp.where(n_b > b, a_lo + v2[b], -jnp.inf) for b in range(k)]
    single = a_hi + v2[0]
    merged, x = [], single
    for b in range(k):
        merged.append(jnp.maximum(lists[b], x))
        x = jnp.minimum(lists[b], x)
    best = _merge_sublanes(merged)
    cmax, tau = best[0], best[-1]
    z = functools.reduce(jnp.add, [jnp.exp(t - cmax) for t in best])
    count_lo = functools.reduce(jnp.add, [jnp.where(t >= tau, 1.0, 0.0) for t in lists])
    count_hi = jnp.where(single >= tau, 1.0, 0.0)
    n1 = jnp.zeros(s1.shape, F32)
    for a in reversed(range(k)):
        r = a % SUBLANES
        n_a = (count_lo if a < SUBLANES else count_hi)[r:r + 1]
        n1 = jnp.where(s1 == v1[a][0:1], n_a, n1)
    rank2 = functools.reduce(jnp.add, [jnp.where(v2[b][0:1] > s2, 1.0, 0.0) for b in range(k)])
    n1_ref[0] = _bf16_pair(n1)
    e1_ref[0] = _bf16_pair(0.5 * jnp.exp(s1 - v1[0][0:1]) / z[0:1])
    r2_ref[0] = _bf16_pack_rows(rank2)
    e2_ref[0] = _bf16_pack_rows(jnp.exp(s2 - v2[0][0:1]))


def _peer_topk(q, sub_k):
    n = q.shape[0]
    tt = 256
    assert n % tt == 0
    row = pl.BlockSpec((1, PEER_KEYS, tt), lambda i, h: (h, 0, i))
    tile = pl.BlockSpec((1, PEER_KEYS // 2, tt), lambda i, h: (h, 0, i))
    row_shape = jax.ShapeDtypeStruct((PEER_HEADS, PEER_KEYS, n), jnp.uint32)
    tile_shape = jax.ShapeDtypeStruct((PEER_HEADS, PEER_KEYS // 2, n), jnp.uint32)
    return pl.pallas_call(
        _peer_topk_kernel,
        grid=(n // tt, PEER_HEADS),
        in_specs=[pl.BlockSpec((tt, PEER_DKEY), lambda i, h: (i, h)),
                  pl.BlockSpec((PEER_KEYS, PEER_DKEY // 2), lambda i, h: (0, 0)),
                  pl.BlockSpec((PEER_KEYS, PEER_DKEY // 2), lambda i, h: (0, 0))],
        out_specs=[row, row, tile, tile],
        out_shape=[row_shape, row_shape, tile_shape, tile_shape],
        compiler_params=_params("parallel", "parallel"),
        name="peer_topk",
    )(q, sub_k[0], sub_k[1])


PEER_HALF_I1 = 4
PEER_HALF = PEER_HALF_I1 * PEER_KEYS


def _peer_dense_kernel(xn_ref, u_ref, vta_ref, vtb_ref, n1_ref, e1_ref, r2_ref, e2_ref, h_ref, gout_ref,
                       o_ref, a_ref, wa_ref, wb_ref, acc_ref, *, tt, final_norm):
    e = pl.program_id(1)
    last = pl.num_programs(1) - 1

    @pl.when(e == 0)
    def _():
        acc_ref[...] = jnp.zeros_like(acc_ref)
        wb_ref[...] = jnp.zeros_like(wb_ref)

    live = jnp.where(e < last, jnp.uint32(1), jnp.uint32(0))
    n_tiles = PEER_KEYS // BF16_ROWS

    def packed_row(ref, hd, r, lanes, scale=None):
        word = ref[hd, r:r + 1, lanes]
        if scale is not None:
            word = word * scale
        return pltpu.bitcast(jnp.broadcast_to(word, (8, 128)), BF16)[None]

    def as_bf16_tiles(words):
        return pltpu.bitcast(words.reshape(n_tiles, 8, 128), BF16)

    def build(half, w_ref):
        rows_u = slice(half * PEER_HALF, (half + 1) * PEER_HALF)
        a_ref[...] = lax.dot_general(u_ref[rows_u, :], xn_ref[...], _NT, preferred_element_type=F32)
        pair = 2
        for lg in range(tt // 128):
            lanes = slice(lg * 128, (lg + 1) * 128)
            for il0 in range(0, PEER_HALF_I1, pair):
                gates = [jnp.zeros((n_tiles, BF16_ROWS, 128), BF16) for _ in range(pair)]
                for hd in range(PEER_HEADS):
                    r2 = as_bf16_tiles(r2_ref[hd, :, lanes])
                    e2 = as_bf16_tiles(e2_ref[hd, :, lanes])
                    for k in range(pair):
                        r = half * PEER_HALF_I1 + il0 + k
                        n1b = packed_row(n1_ref, hd, r, lanes, live)
                        e1b = packed_row(e1_ref, hd, r, lanes)
                        gates[k] = gates[k] + jnp.where(r2 < n1b, e2 * e1b, jnp.zeros_like(e2))
                for k in range(pair):
                    rows = slice((il0 + k) * PEER_KEYS, (il0 + k + 1) * PEER_KEYS)
                    gate = _bf16_unpack_rows(pltpu.bitcast(gates[k], jnp.uint32))
                    act = _gelu_tanh_doubled(a_ref[rows, lanes]).reshape(n_tiles, 2, 8, 128)
                    w_ref[rows, lanes] = (act * gate).reshape(PEER_KEYS, 128).astype(BF16)

    acc_ref[...] += jnp.dot(vtb_ref[...], wb_ref[...], preferred_element_type=F32)
    build(0, wa_ref)
    acc_ref[...] += jnp.dot(vta_ref[...], wa_ref[...], preferred_element_type=F32)
    build(1, wb_ref)

    @pl.when(e == last)
    def _():
        out = h_ref[...] + acc_ref[...].T
        if final_norm:
            out = _rms(out, gout_ref[...])
        o_ref[...] = out


def _peer_dense(xn_bf16, u_bf16, vt_bf16, n1, e1, r2, e2, h, g_out, final_norm):
    n = h.shape[0]
    tt = min(512, n)
    n_exp = u_bf16.shape[0]
    n_e = n_exp // (2 * PEER_HALF)
    tok = lambda i, e: (i, 0)
    cur = lambda e: jnp.minimum(e, n_e - 1)
    return pl.pallas_call(
        functools.partial(_peer_dense_kernel, tt=tt, final_norm=final_norm),
        grid=(n // tt, n_e + 1),
        in_specs=[pl.BlockSpec((tt, D_MODEL), tok),
                  pl.BlockSpec((2 * PEER_HALF, D_MODEL), lambda i, e: (cur(e), 0)),
                  pl.BlockSpec((D_MODEL, PEER_HALF), lambda i, e: (0, 2 * cur(e))),
                  pl.BlockSpec((D_MODEL, PEER_HALF), lambda i, e: (0, jnp.maximum(2 * e - 1, 0))),
                  pl.BlockSpec((PEER_HEADS, 2 * PEER_HALF_I1, tt), lambda i, e: (0, cur(e), i)),
                  pl.BlockSpec((PEER_HEADS, 2 * PEER_HALF_I1, tt), lambda i, e: (0, cur(e), i)),
                  pl.BlockSpec((PEER_HEADS, PEER_KEYS // 2, tt), lambda i, e: (0, 0, i)),
                  pl.BlockSpec((PEER_HEADS, PEER_KEYS // 2, tt), lambda i, e: (0, 0, i)),
                  pl.BlockSpec((tt, D_MODEL), tok),
                  pl.BlockSpec((1, D_MODEL), lambda i, e: (0, 0))],
        out_specs=pl.BlockSpec((tt, D_MODEL), tok),
        out_shape=jax.ShapeDtypeStruct((n, D_MODEL), F32),
        scratch_shapes=[pltpu.VMEM((PEER_HALF, tt), F32), pltpu.VMEM((PEER_HALF, tt), BF16),
                        pltpu.VMEM((PEER_HALF, tt), BF16), pltpu.VMEM((D_MODEL, tt), F32)],
        compiler_params=_params("parallel", "arbitrary"),
        name="peer_dense",
    )(xn_bf16, u_bf16, vt_bf16, vt_bf16, n1, e1, r2, e2, h, g_out.reshape(1, D_MODEL))


def _transpose_kernel(v_ref, o_ref):
    o_ref[...] = v_ref[...].T.astype(o_ref.dtype)


def _transpose_bf16(v):
    r, c = v.shape
    tr = 512
    return pl.pallas_call(
        _transpose_kernel,
        grid=(r // tr,),
        in_specs=[pl.BlockSpec((tr, c), lambda i: (i, 0))],
        out_specs=pl.BlockSpec((c, tr), lambda i: (0, i)),
        out_shape=jax.ShapeDtypeStruct((c, r), BF16),
        compiler_params=_params("parallel"),
        name="transpose_bf16",
    )(v)


def _peer_layer(h, g_norm, w_q, sub_k, u_emb, v_emb, g_out, final_norm):
    q, xn = _norm_matmul(h, g_norm, w_q.astype(BF16), "peer")
    n1, e1, r2, e2 = _peer_topk(q, sub_k)
    return _peer_dense(xn, u_emb.astype(BF16), _transpose_bf16(v_emb), n1, e1, r2, e2, h, g_out, final_norm)


GATHER_PAGES = 8
GATHER_ROWS = GATHER_PAGES * PAGE_SIZE


def _gather_kernel(pt_ref, *refs, n_steps):
    page_refs, new_ref, o_ref = refs[:GATHER_PAGES], refs[GATHER_PAGES], refs[GATHER_PAGES + 1]
    p = pl.program_id(1)

    @pl.when(p < n_steps)
    def _():
        for k, page_ref in enumerate(page_refs):
            o_ref[0, k * PAGE_SIZE:(k + 1) * PAGE_SIZE, :] = page_ref[0].T.astype(o_ref.dtype)

    @pl.when(p >= n_steps)
    def _():
        o_ref[0] = new_ref[0].astype(o_ref.dtype)


def _gather_pages(cache, page_table, new_rows, out_dtype):
    b, n_pages = page_table.shape
    assert n_pages % GATHER_PAGES == 0 and new_rows.shape[1] <= GATHER_ROWS
    n_steps = n_pages // GATHER_PAGES
    new_pad = jnp.pad(new_rows, ((0, 0), (0, GATHER_ROWS - new_rows.shape[1]), (0, 0)))

    def page_spec(k):
        return pl.BlockSpec((1, KV_COLS, PAGE_SIZE),
                            lambda i, p, pt: (pt[i, jnp.minimum(p, n_steps - 1) * GATHER_PAGES + k], 0, 0))

    grid_spec = pltpu.PrefetchScalarGridSpec(
        num_scalar_prefetch=1,
        grid=(b, n_steps + 1),
        in_specs=[page_spec(k) for k in range(GATHER_PAGES)]
        + [pl.BlockSpec((1, GATHER_ROWS, KV_COLS), lambda i, p, pt: (i, 0, 0))],
        out_specs=pl.BlockSpec((1, GATHER_ROWS, KV_COLS), lambda i, p, pt: (i, p, 0)),
    )
    return pl.pallas_call(
        functools.partial(_gather_kernel, n_steps=n_steps),
        grid_spec=grid_spec,
        out_shape=jax.ShapeDtypeStruct((b, (n_steps + 1) * GATHER_ROWS, KV_COLS), out_dtype),
        compiler_params=_params("parallel", "arbitrary"),
        name="gather_pages",
    )(page_table, *([cache] * GATHER_PAGES), new_pad)


CMP_TILE = 128


CMP_TOKENS = CMP_TILE * CMP_STRIDE
CMP_LANE_TILES = KV_COLS // 128
CMP_PAGES = CMP_TOKENS // PAGE_SIZE


def _compress_kernel(main_ref, halo_ref, pe_ref, w1k_ref, w1v_ref, w2k_ref, w2v_ref, o_ref, win_ref):
    for c in range(CMP_LANE_TILES):
        win_ref[c, 0:CMP_TOKENS, :] = main_ref[0, :, c * 128:(c + 1) * 128]
        win_ref[c, CMP_TOKENS:CMP_TOKENS + CMP_STRIDE, :] = halo_ref[0, :, c * 128:(c + 1) * 128]
    _compress_window(win_ref, pe_ref, w1k_ref, w1v_ref, w2k_ref, w2v_ref, o_ref)


def _compress_paged_kernel(pt_ref, *refs):
    page_refs, halo_ref = refs[:CMP_PAGES], refs[CMP_PAGES]
    pe_ref, w1k_ref, w1v_ref, w2k_ref, w2v_ref, o_ref, win_ref = refs[CMP_PAGES + 1:]
    for c in range(CMP_LANE_TILES):
        cols = slice(c * 128, (c + 1) * 128)
        for k, page_ref in enumerate(page_refs):
            win_ref[c, k * PAGE_SIZE:(k + 1) * PAGE_SIZE, :] = page_ref[0, cols, :].T
        win_ref[c, CMP_TOKENS:CMP_TOKENS + CMP_STRIDE, :] = halo_ref[0, cols, :].T[0:CMP_STRIDE]
    _compress_window(win_ref, pe_ref, w1k_ref, w1v_ref, w2k_ref, w2v_ref, o_ref)


def _compress_window(win_ref, pe_ref, w1k_ref, w1v_ref, w2k_ref, w2v_ref, o_ref):
    n_lane_tiles = CMP_LANE_TILES
    hk = jnp.zeros((CMP_TILE, N_KV_HEADS * CMP_HIDDEN), F32)
    hv = jnp.zeros((CMP_TILE, N_KV_HEADS * CMP_HIDDEN), F32)
    for s in range(CMP_BLOCK):
        x = jnp.concatenate([win_ref[c, pl.ds(s, CMP_TILE, stride=CMP_STRIDE), :] for c in range(n_lane_tiles)],
                            axis=1)
        x = (x + pe_ref[s]).astype(BF16)
        hk = hk + jnp.dot(x[:, :K_COLS], w1k_ref[s], preferred_element_type=F32)
        hv = hv + jnp.dot(x[:, K_COLS:], w1v_ref[s], preferred_element_type=F32)
    o_ref[0, :, :K_COLS] = jnp.dot(_gelu_tanh(hk).astype(BF16), w2k_ref[...], preferred_element_type=F32)
    o_ref[0, :, K_COLS:] = jnp.dot(_gelu_tanh(hv).astype(BF16), w2v_ref[...], preferred_element_type=F32)


def _block_diag(w):
    eye = jnp.eye(N_KV_HEADS, dtype=w.dtype)
    out = jnp.einsum("gh,...ab->...gahb", eye, w)
    return out.reshape(w.shape[:-2] + (N_KV_HEADS * w.shape[-2], N_KV_HEADS * w.shape[-1]))


def _compress(rows, n_tiles, cmp_pe, cmp_w1, cmp_w2):
    b, length = rows.shape[:2]
    last_halo = length // CMP_STRIDE - 1
    weights, weight_specs = _compress_weights(cmp_pe, cmp_w1, cmp_w2, lambda shape: (lambda i, j: (0,) * len(shape)))
    return pl.pallas_call(
        _compress_kernel,
        grid=(b, n_tiles),
        in_specs=[pl.BlockSpec((1, CMP_TOKENS, KV_COLS), lambda i, j: (i, j, 0)),
                  pl.BlockSpec((1, CMP_STRIDE, KV_COLS),
                               lambda i, j: (i, jnp.minimum((j + 1) * CMP_TILE, last_halo), 0))] + weight_specs,
        out_specs=pl.BlockSpec((1, CMP_TILE, KV_COLS), lambda i, j: (i, j, 0)),
        out_shape=jax.ShapeDtypeStruct((b, n_tiles * CMP_TILE, KV_COLS), F32),
        scratch_shapes=[pltpu.VMEM((CMP_LANE_TILES, CMP_TOKENS + CMP_STRIDE, 128), F32)],
        compiler_params=_params("parallel", "parallel"),
        name="cmp_compress",
    )(rows, rows, *weights)


def _compress_weights(cmp_pe, cmp_w1, cmp_w2, const_map):
    w1 = cmp_w1.reshape(2, CMP_BLOCK, HEAD_DIM, CMP_HIDDEN)
    w1k, w1v = _block_diag(w1[0]).astype(BF16), _block_diag(w1[1]).astype(BF16)
    w2k, w2v = _block_diag(cmp_w2[0]).astype(BF16), _block_diag(cmp_w2[1]).astype(BF16)
    pe = jnp.concatenate([jnp.tile(cmp_pe[0], (1, N_KV_HEADS)), jnp.tile(cmp_pe[1], (1, N_KV_HEADS))], axis=1)
    weights = [pe.reshape(CMP_BLOCK, 1, KV_COLS), w1k, w1v, w2k, w2v]
    return weights, [pl.BlockSpec(w.shape, const_map(w.shape)) for w in weights]


def _compress_paged(cache, page_table, n_tiles, cmp_pe, cmp_w1, cmp_w2):
    b, n_pages = page_table.shape
    assert n_tiles * CMP_PAGES <= n_pages
    weights, weight_specs = _compress_weights(cmp_pe, cmp_w1, cmp_w2,
                                              lambda shape: (lambda i, j, pt: (0,) * len(shape)))

    def page_spec(k):
        return pl.BlockSpec((1, KV_COLS, PAGE_SIZE),
                            lambda i, j, pt: (pt[i, jnp.minimum(j * CMP_PAGES + k, n_pages - 1)], 0, 0))

    grid_spec = pltpu.PrefetchScalarGridSpec(
        num_scalar_prefetch=1,
        grid=(b, n_tiles),
        in_specs=[page_spec(k) for k in range(CMP_PAGES + 1)] + weight_specs,
        out_specs=pl.BlockSpec((1, CMP_TILE, KV_COLS), lambda i, j, pt: (i, j, 0)),
        scratch_shapes=[pltpu.VMEM((CMP_LANE_TILES, CMP_TOKENS + CMP_STRIDE, 128), F32)],
    )
    return pl.pallas_call(
        _compress_paged_kernel,
        grid_spec=grid_spec,
        out_shape=jax.ShapeDtypeStruct((b, n_tiles * CMP_TILE, KV_COLS), F32),
        compiler_params=_params("parallel", "parallel"),
        name="cmp_compress_paged",
    )(page_table, *([cache] * (CMP_PAGES + 1)), *weights)


def _head_perm():
    c = np.arange(D_MODEL)
    h, g, d = c // K_COLS, (c % K_COLS) // HEAD_DIM, c % HEAD_DIM
    return (g * HEADS_PER_KV + h) * HEAD_DIM + d


def _gate_expand(branch):
    c = np.arange(D_MODEL)
    h, g = c // K_COLS, (c % K_COLS) // HEAD_DIM
    m = np.zeros((GATE_COLS, D_MODEL), np.float32)
    m[(g * HEADS_PER_KV + h) * N_BRANCH + branch, c] = 1.0
    return jnp.asarray(m)


def _block_diag_queries(q):
    tq = q.shape[0]
    lane_g = lax.broadcasted_iota(jnp.int32, (tq, K_COLS), 1) // HEAD_DIM
    blocks = []
    for g in range(N_KV_HEADS):
        for h in range(HEADS_PER_KV):
            blocks.append(jnp.where(lane_g == g, q[:, h * K_COLS:(h + 1) * K_COLS], 0.0))
    return jnp.concatenate(blocks, axis=0)


def _diag_heads(acc, tq):
    lane_g = lax.broadcasted_iota(jnp.int32, (tq, K_COLS), 1) // HEAD_DIM
    outs = []
    for h in range(HEADS_PER_KV):
        o = jnp.zeros((tq, K_COLS), F32)
        for g in range(N_KV_HEADS):
            r0 = (g * HEADS_PER_KV + h) * tq
            o = o + jnp.where(lane_g == g, acc[r0:r0 + tq], 0.0)
        outs.append(o)
    return jnp.concatenate(outs, axis=1)


_NT = (((1,), (1,)), ((), ()))


def _cmp_attn_kernel(q_ref, gate_ref, kvc_ref, ovt_ref, eg_ref, oc_ref, sel_ref, sc_ref,
                     *, tq, pos0, n_slc_loop):
    q0 = pos0 + pl.program_id(1) * tq
    n_pad = kvc_ref.shape[1]
    s_pad = ovt_ref.shape[0]
    rows = N_HEADS * tq
    qbd = _block_diag_queries(q_ref[0]).astype(BF16)
    kc = kvc_ref[0, :, :K_COLS].astype(BF16)
    vc = kvc_ref[0, :, K_COLS:].astype(BF16)
    s = lax.dot_general(qbd, kc, _NT, preferred_element_type=F32)
    t_row = q0 + (lax.broadcasted_iota(jnp.int32, (rows, n_pad), 0) & (tq - 1))
    n_col = lax.broadcasted_iota(jnp.int32, (rows, n_pad), 1)
    vis = (n_col * CMP_STRIDE + (CMP_BLOCK - 1)) <= t_row
    m = jnp.max(jnp.where(vis, s, -jnp.inf), axis=-1, keepdims=True)
    m = jnp.where(m > -jnp.inf, m, 0.0)
    e = jnp.where(vis, jnp.exp(s - m), 0.0)
    p = e / jnp.maximum(jnp.sum(e, axis=-1, keepdims=True), 1e-30)
    oc = _diag_heads(jnp.dot(p.astype(BF16), vc, preferred_element_type=F32), tq)
    oc_ref[0] = oc * jnp.dot(gate_ref[0], eg_ref[...], preferred_element_type=F32)

    psum = jnp.concatenate(
        [sum(p[(g * HEADS_PER_KV + h) * tq:(g * HEADS_PER_KV + h + 1) * tq] for h in range(HEADS_PER_KV))
         for g in range(N_KV_HEADS)], axis=0)
    imp_t = lax.dot_general(ovt_ref[...], psum, _NT, preferred_element_type=F32)
    cols = N_KV_HEADS * tq
    blk = lax.broadcasted_iota(jnp.int32, (s_pad, cols), 0)
    cur = (q0 + (lax.broadcasted_iota(jnp.int32, (s_pad, cols), 1) & (tq - 1))) // SLC_BLOCK
    forced = (blk == 0) | (blk == cur) | (blk == cur - 1)
    score = jnp.where(forced, jnp.inf, jnp.where(blk <= cur, imp_t, -jnp.inf))
    sc_ref[...] = score

    score = score[:n_slc_loop]
    blk = lax.broadcasted_iota(jnp.int32, (n_slc_loop, cols), 0)
    cur = (q0 + (lax.broadcasted_iota(jnp.int32, (n_slc_loop, cols), 1) & (tq - 1))) // SLC_BLOCK

    def rank_body(jb, rank):
        rows8 = sc_ref[pl.ds(pl.multiple_of(jb * 8, 8), 8), :]
        for r in range(8):
            row = rows8[r:r + 1]
            tie = jnp.where(row == score, jnp.where(jb * 8 + r < blk, 1.0, 0.0), 0.0)
            rank = rank + jnp.where(row > score, 1.0, tie)
        return rank

    rank = lax.fori_loop(0, n_slc_loop // 8, rank_body, jnp.zeros((n_slc_loop, cols), F32))
    sel_t = jnp.where(rank < float(N_SELECT), jnp.where(blk <= cur, 1.0, 0.0), 0.0)
    if n_slc_loop < s_pad:
        sel_t = jnp.concatenate([sel_t, jnp.zeros((s_pad - n_slc_loop, cols), F32)], axis=0)
    sel_t = sel_t.astype(BF16)
    ci = lax.broadcasted_iota(jnp.int32, (cols, cols), 0)
    cj = lax.broadcasted_iota(jnp.int32, (cols, cols), 1)
    eye = jnp.where(ci == cj, 1.0, 0.0).astype(BF16)
    sel_ref[0, 0] = lax.dot_general(eye, sel_t, _NT, preferred_element_type=F32)


def _cmp_attn(q, gates, kvc, ov_t, pos0, tq, n_slc):
    b, t, _ = q.shape
    s_pad = ov_t.shape[0]
    n_pad = kvc.shape[1]
    n_slc_loop = -(-n_slc // 8) * 8
    return pl.pallas_call(
        functools.partial(_cmp_attn_kernel, tq=tq, pos0=pos0, n_slc_loop=n_slc_loop),
        grid=(b, t // tq),
        in_specs=[pl.BlockSpec((1, tq, D_MODEL), lambda i, j: (i, j, 0)),
                  pl.BlockSpec((1, tq, GATE_COLS), lambda i, j: (i, j, 0)),
                  pl.BlockSpec((1, n_pad, KV_COLS), lambda i, j: (i, 0, 0)),
                  pl.BlockSpec((s_pad, n_pad), lambda i, j: (0, 0)),
                  pl.BlockSpec((GATE_COLS, D_MODEL), lambda i, j: (0, 0))],
        out_specs=[pl.BlockSpec((1, tq, D_MODEL), lambda i, j: (i, j, 0)),
                   pl.BlockSpec((1, 1, N_KV_HEADS * tq, s_pad), lambda i, j: (i, j, 0, 0))],
        out_shape=[jax.ShapeDtypeStruct((b, t, D_MODEL), F32),
                   jax.ShapeDtypeStruct((b, t // tq, N_KV_HEADS * tq, s_pad), F32)],
        scratch_shapes=[pltpu.VMEM((s_pad, N_KV_HEADS * tq), F32)],
        compiler_params=_params("parallel", "parallel"),
        name="nsa_cmp_select",
    )(q, gates, kvc, ov_t, _gate_expand(0))


WIN_KEYS = 768
ATTN_ROWS = 64


def _lane_tiles(x):
    return [x[:, i * 128:(i + 1) * 128] for i in range(x.shape[1] // 128)]


def _row_max(x):
    return jnp.max(functools.reduce(jnp.maximum, _lane_tiles(x)), axis=-1, keepdims=True)


def _row_sum(x):
    return jnp.sum(functools.reduce(jnp.add, _lane_tiles(x)), axis=-1, keepdims=True)


def _sparse_attn_kernel(q_ref, gate_ref, sel_ref, oc_ref, kslc_ref, kwin_ref, egs_ref, egw_ref, o_ref,
                        qaug_ref, s_ref, p_ref, sw_ref, pw_ref, m_ref, l_ref, alpha_ref, acc_ref,
                        *, tq, tk, pos0, win_pos0):
    q0 = pos0 + pl.program_id(1) * tq
    rows = N_HEADS * tq
    s_pad = sel_ref.shape[3]
    rb = min(ATTN_ROWS, rows)

    qaug_ref[:, 0:K_COLS] = _block_diag_queries(q_ref[0]).astype(BF16)
    not_sel = sel_ref[0, 0] - 1.0
    qaug_ref[:, K_COLS:] = jnp.concatenate(
        [not_sel[g * tq:(g + 1) * tq] for g in range(N_KV_HEADS) for _ in range(HEADS_PER_KV)],
        axis=0).astype(BF16)

    def q_pos(r0, width):
        return q0 + ((r0 + lax.broadcasted_iota(jnp.int32, (rb, width), 0)) & (tq - 1))

    m_ref[...] = jnp.full_like(m_ref, NEG_BIG)
    l_ref[...] = jnp.zeros_like(l_ref)
    acc_ref[...] = jnp.zeros_like(acc_ref)
    blk_lane = lax.broadcasted_iota(jnp.int32, (tk, s_pad), 1)
    blk_key = lax.broadcasted_iota(jnp.int32, (tk, s_pad), 0) // SLC_BLOCK
    k_col = lax.broadcasted_iota(jnp.int32, (rb, tk), 1)

    n_split = s_ref.shape[0]
    part = rows // n_split

    def slc_tile(j, causal):
        ks = pl.ds(pl.multiple_of(j * tk, tk), tk)
        onehot = jnp.where(blk_lane == j * (tk // SLC_BLOCK) + blk_key, -NEG_BIG, 0.0).astype(BF16)
        kaug = jnp.concatenate([kslc_ref[0, ks, 0:K_COLS], onehot], axis=1)
        vt = kslc_ref[0, ks, K_COLS:KV_COLS]
        for sp in range(n_split):
            s_ref[sp] = lax.dot_general(qaug_ref[sp * part:(sp + 1) * part, :], kaug, _NT,
                                        preferred_element_type=F32)
        for sp in range(n_split):
            for r0 in range(0, part, rb):
                blk = slice(sp * part + r0, sp * part + r0 + rb)
                s = s_ref[sp, r0:r0 + rb, :]
                if causal:
                    s = jnp.where(j * tk + k_col <= q_pos(sp * part + r0, tk), s, NEG_BIG)
                m_old = m_ref[blk, :]
                m_new = jnp.maximum(m_old, _row_max(s))
                p = jnp.exp(s - m_new)
                alpha = jnp.exp(m_old - m_new)
                l_ref[blk, :] = alpha * l_ref[blk, :] + _row_sum(p)
                m_ref[blk, :] = m_new
                alpha_ref[blk, :] = alpha
                p_ref[sp, r0:r0 + rb, :] = p.astype(BF16)
            prows = slice(sp * part, (sp + 1) * part)
            acc_ref[prows, :] = alpha_ref[prows, :] * acc_ref[prows, :] + jnp.dot(
                p_ref[sp], vt, preferred_element_type=F32)

    last_tile = (q0 + tq - 1) // tk
    lax.fori_loop(0, last_tile, lambda j, c: (slc_tile(j, False), c)[1], 0)
    slc_tile(last_tile, True)
    o_s = _diag_heads(acc_ref[...] / jnp.maximum(l_ref[...], 1e-30), tq)

    lw = kwin_ref.shape[1]
    w0 = jnp.clip((q0 - win_pos0 - WINDOW) // 128 * 128, 0, lw - WIN_KEYS)
    wk = pl.ds(pl.multiple_of(w0, 128), WIN_KEYS)
    kw = kwin_ref[0, wk, 0:K_COLS]
    vw = kwin_ref[0, wk, K_COLS:KV_COLS]
    for sp in range(n_split):
        sw_ref[sp] = lax.dot_general(qaug_ref[sp * part:(sp + 1) * part, 0:K_COLS], kw, _NT,
                                     preferred_element_type=F32)
    kpos = win_pos0 + w0 + lax.broadcasted_iota(jnp.int32, (rb, WIN_KEYS), 1)
    for sp in range(n_split):
        for r0 in range(0, part, rb):
            blk = slice(sp * part + r0, sp * part + r0 + rb)
            t = q_pos(sp * part + r0, WIN_KEYS)
            s = jnp.where((kpos <= t) & (kpos > t - WINDOW), sw_ref[sp, r0:r0 + rb, :], NEG_BIG)
            p = jnp.exp(s - _row_max(s))
            l_ref[blk, :] = _row_sum(p)
            pw_ref[sp, r0:r0 + rb, :] = p.astype(BF16)
        prows = slice(sp * part, (sp + 1) * part)
        acc_ref[prows, :] = jnp.dot(pw_ref[sp], vw, preferred_element_type=F32)
    o_w = _diag_heads(acc_ref[...] / jnp.maximum(l_ref[...], 1e-30), tq)

    gate = gate_ref[0]
    g_s = jnp.dot(gate, egs_ref[...], preferred_element_type=F32)
    g_w = jnp.dot(gate, egw_ref[...], preferred_element_type=F32)
    o_ref[0] = (oc_ref[0] + g_s * o_s + g_w * o_w).astype(o_ref.dtype)


def _sparse_attn(q, gates, sel, oc, kslc, kwin, slc_col, win_col, pos0, win_pos0, tq, tk):
    b, t, _ = q.shape
    s_pad = sel.shape[3]
    ls, lw = kslc.shape[1], kwin.shape[1]
    assert ls % tk == 0 and lw % 128 == 0 and lw >= WIN_KEYS and tk % tq == 0
    rows = N_HEADS * tq
    n_split = max(1, min(4, rows // (2 * ATTN_ROWS)))
    qtile = lambda i, j: (i, j, 0)
    return pl.pallas_call(
        functools.partial(_sparse_attn_kernel, tq=tq, tk=tk, pos0=pos0, win_pos0=win_pos0),
        grid=(b, t // tq),
        in_specs=[pl.BlockSpec((1, tq, D_MODEL), qtile),
                  pl.BlockSpec((1, tq, GATE_COLS), qtile),
                  pl.BlockSpec((1, 1, N_KV_HEADS * tq, s_pad), lambda i, j: (i, j, 0, 0)),
                  pl.BlockSpec((1, tq, D_MODEL), qtile),
                  pl.BlockSpec((1, ls, KV_COLS), lambda i, j: (i, 0, slc_col)),
                  pl.BlockSpec((1, lw, KV_COLS), lambda i, j: (i, 0, win_col)),
                  pl.BlockSpec((GATE_COLS, D_MODEL), lambda i, j: (0, 0)),
                  pl.BlockSpec((GATE_COLS, D_MODEL), lambda i, j: (0, 0))],
        out_specs=pl.BlockSpec((1, tq, D_MODEL), qtile),
        out_shape=jax.ShapeDtypeStruct((b, t, D_MODEL), F32),
        scratch_shapes=[pltpu.VMEM((rows, K_COLS + s_pad), BF16),
                        pltpu.VMEM((n_split, rows // n_split, tk), F32),
                        pltpu.VMEM((n_split, rows // n_split, tk), BF16),
                        pltpu.VMEM((n_split, rows // n_split, WIN_KEYS), F32),
                        pltpu.VMEM((n_split, rows // n_split, WIN_KEYS), BF16),
                        pltpu.VMEM((rows, 1), F32), pltpu.VMEM((rows, 1), F32), pltpu.VMEM((rows, 1), F32),
                        pltpu.VMEM((rows, K_COLS), F32)],
        compiler_params=_params("parallel", "arbitrary"),
        name="nsa_select_window",
    )(q, gates, sel, oc, kslc, kwin, _gate_expand(1), _gate_expand(2))


ATTN_TK = 512


def _overlap_t(s_pad, n_pad, n_slc, n_cmp):
    n = np.arange(n_pad)[None, :]
    s = np.arange(s_pad)[:, None]
    ov = ((n * CMP_STRIDE < s * SLC_BLOCK + SLC_BLOCK) & (n * CMP_STRIDE + CMP_BLOCK - 1 >= s * SLC_BLOCK)
          & (n < n_cmp) & (s < n_slc))
    return jnp.asarray(ov.astype(np.float32))


def _trunk(x, conv_past, past, pos0, p):
    b, t, _ = x.shape
    n = b * t
    kvh = (2, N_KV_HEADS, HEAD_DIM)
    h = x.reshape(n, D_MODEL)

    (u,) = _norm_matmul(h, p["norm_mix_g"][0], p["conv_w_in"][0].astype(BF16), "glu")
    u = u.reshape(b, t, D_MODEL)
    h = _conv_module(u, conv_past, h.reshape(b, t, D_MODEL), p["conv_dw_k"][0], p["conv_dw_b"][0],
                     p["conv_ln_g"][0], p["conv_ln_b"][0], p["conv_w_out"][0].astype(BF16)).reshape(n, D_MODEL)
    conv_state = jnp.concatenate([conv_past, u], axis=1)[:, -(CONV_WIDTH - 1):][None]
    h = _peer_layer(h, p["norm_ffn_g"][0], p["peer_w_q"][0], p["peer_sub_k"][0], p["peer_u"][0],
                    p["peer_v"][0], p["norm_out_g"], False)

    kv, kv_bf16 = _norm_matmul(h, p["norm_kv_g"], p["nsa_w_kv"].astype(BF16), "kv")
    kv = kv.reshape(b, t, N_BRANCH * KV_COLS)
    kv_bf16 = kv_bf16.reshape(b, t, N_BRANCH * KV_COLS)
    new_cmp, new_slc, new_win = (kv[..., i * KV_COLS:(i + 1) * KV_COLS] for i in range(N_BRANCH))
    if past is None:
        length, win_pos0 = t, pos0
        cmp_rows, k_slc, k_win, slc_col, win_col = kv, kv_bf16, kv_bf16, 1, 2
        win_all = new_win
    else:
        cache_cmp, cache_slc, cache_win, page_table = past
        length, win_pos0 = pos0 + t, pos0 - cache_win.shape[1]
        k_slc = _gather_pages(cache_slc, page_table, new_slc, BF16)
        win_all = jnp.concatenate([cache_win, new_win], axis=1)
        lw = max(WIN_KEYS, -(-win_all.shape[1] // 128) * 128)
        k_win = jnp.pad(win_all, ((0, 0), (0, lw - win_all.shape[1]), (0, 0))).astype(BF16)
        slc_col, win_col = 0, 0
    n_cmp = (length - CMP_BLOCK) // CMP_STRIDE + 1
    n_slc = -(-length // SLC_BLOCK)
    n_tiles = -(-n_cmp // CMP_TILE)
    if past is None:
        kvc = _compress(cmp_rows, n_tiles, p["nsa_cmp_pe"], p["nsa_cmp_w1"], p["nsa_cmp_w2"])
    else:
        assert (n_cmp - 1) * CMP_STRIDE + CMP_BLOCK <= page_table.shape[1] * PAGE_SIZE
        kvc = _compress_paged(cache_cmp, page_table, n_tiles, p["nsa_cmp_pe"], p["nsa_cmp_w1"], p["nsa_cmp_w2"])

    perm = _head_perm()
    w_in = p["nsa_w_in"][0]
    w_in = jnp.concatenate([w_in[:, perm], w_in[:, D_MODEL:],
                            jnp.zeros((D_MODEL, GATE_COLS - N_HEADS * N_BRANCH), F32)], axis=1).astype(BF16)
    q, gates = _norm_matmul(h, p["norm_mix_g"][1], w_in, "nsa")
    q = q.reshape(b, t, D_MODEL)
    gates = gates.reshape(b, t, GATE_COLS)
    tq = min(64, t)
    s_pad = -(-n_slc // 128) * 128
    oc, sel = _cmp_attn(q, gates, kvc, _overlap_t(s_pad, kvc.shape[1], n_slc, n_cmp), pos0, tq, n_slc)
    o = _sparse_attn(q, gates, sel, oc, k_slc, k_win, slc_col, win_col, pos0, win_pos0, tq, ATTN_TK)
    h = _matmul_res(o.reshape(n, D_MODEL), p["nsa_w_o"][0][perm, :].astype(BF16), h)
    y = _peer_layer(h, p["norm_ffn_g"][1], p["peer_w_q"][1], p["peer_sub_k"][1], p["peer_u"][1],
                    p["peer_v"][1], p["norm_out_g"], True)
    keep = min(WINDOW, win_all.shape[1])
    shape5 = lambda a: a.reshape(a.shape[:2] + kvh)
    return (y.reshape(b, t, D_MODEL), conv_state, shape5(new_cmp), shape5(new_slc), shape5(win_all[:, -keep:]))


def kernel(x_prompt, x_sample, state_conv, cache_cmp_kv, cache_slc_kv, cache_win_kv, page_table, norm_mix_g, norm_ffn_g, norm_kv_g, norm_out_g, conv_w_in, conv_dw_k, conv_dw_b, conv_ln_g, conv_ln_b, conv_w_out, nsa_w_kv, nsa_cmp_pe, nsa_cmp_w1, nsa_cmp_w2, nsa_w_in, nsa_w_o, peer_w_q, peer_sub_k, peer_u, peer_v):
    p = dict(norm_mix_g=norm_mix_g, norm_ffn_g=norm_ffn_g, norm_kv_g=norm_kv_g, norm_out_g=norm_out_g,
             conv_w_in=conv_w_in, conv_dw_k=conv_dw_k, conv_dw_b=conv_dw_b, conv_ln_g=conv_ln_g,
             conv_ln_b=conv_ln_b, conv_w_out=conv_w_out, nsa_w_kv=nsa_w_kv, nsa_cmp_pe=nsa_cmp_pe,
             nsa_cmp_w1=nsa_cmp_w1, nsa_cmp_w2=nsa_cmp_w2, nsa_w_in=nsa_w_in, nsa_w_o=nsa_w_o,
             peer_w_q=peer_w_q, peer_sub_k=peer_sub_k, peer_u=peer_u, peer_v=peer_v)
    bsz = x_prompt.shape[0]
    conv0 = jnp.zeros((bsz, CONV_WIDTH - 1, D_MODEL), x_prompt.dtype)
    y_p, conv_p, cmp_p, slc_p, win_p = _trunk(x_prompt, conv0, None, 0, p)
    dec_b, n_pages = page_table.shape
    flat = lambda c: c.reshape(c.shape[0], c.shape[1], KV_COLS)
    paged = lambda c: c.transpose(0, 2, 3, 4, 1).reshape(c.shape[0], KV_COLS, c.shape[1])
    past = (paged(cache_cmp_kv), paged(cache_slc_kv), flat(cache_win_kv), page_table)
    y_s, conv_s, cmp_s, slc_s, win_s = _trunk(x_sample, state_conv[0], past, n_pages * PAGE_SIZE, p)
    return (y_p, y_s, conv_p, cmp_p, slc_p, win_p, conv_s, cmp_s, slc_s, win_s)
```

```python
import functools
import math

import numpy as np
import jax
import jax.numpy as jnp
from jax import lax
from jax.experimental import pallas as pl
from jax.experimental.pallas import tpu as pltpu

F32 = jnp.float32
BF16 = jnp.bfloat16

D_MODEL = 1024
EPS = 1e-6
CONV_WIDTH = 31
CONV_HALO = 32
CONV_PAD = 8
SUBLANES = 8
N_HEADS = 16
HEAD_DIM = 64
N_KV_HEADS = 4
HEADS_PER_KV = 4
N_BRANCH = 3
KV_COLS = 2 * N_KV_HEADS * HEAD_DIM
K_COLS = N_KV_HEADS * HEAD_DIM
CMP_BLOCK = 32
CMP_STRIDE = 16
CMP_HIDDEN = 128
SLC_BLOCK = 64
N_SELECT = 16
WINDOW = 512
PAGE_SIZE = 128
PEER_HEADS = 8
PEER_KEYS = 128
PEER_TOPK = 16
PEER_DKEY = 256
GATE_COLS = 128

VMEM_LIMIT_BYTES = 56 * 1024 * 1024
NEG_BIG = -1e30


def _params(*sem):
    return pltpu.CompilerParams(dimension_semantics=sem, vmem_limit_bytes=VMEM_LIMIT_BYTES)


def _gelu_tanh(x):
    return 0.5 * x * (1.0 + jnp.tanh(0.7978845608028654 * (x + 0.044715 * (x * x * x))))


def _gelu_tanh_doubled(x):
    inner = x * (0.7978845608028654 * 0.044715 * (x * x) + 0.7978845608028654)
    return x * jnp.tanh(inner) + x


def _sigmoid(x):
    return 1.0 / (1.0 + jnp.exp(-x))


def _rms(x, g):
    return x * lax.rsqrt(jnp.mean(x * x, axis=-1, keepdims=True) + EPS) * g


def _norm_matmul_kernel(x_ref, g_ref, w_ref, *out_refs, mode):
    xn = _rms(x_ref[...], g_ref[...])
    xb = xn.astype(BF16)
    y = jnp.dot(xb, w_ref[...], preferred_element_type=F32)
    if mode == "glu":
        out_refs[0][...] = y[:, :D_MODEL] * _sigmoid(y[:, D_MODEL:])
    elif mode == "peer":
        out_refs[0][...] = y
        out_refs[1][...] = xb
    elif mode == "kv":
        out_refs[0][...] = y
        out_refs[1][...] = y.astype(BF16)
    elif mode == "nsa":
        out_refs[0][...] = y[:, :D_MODEL] * (HEAD_DIM ** -0.5)
        out_refs[1][...] = _sigmoid(y[:, D_MODEL:])
    else:
        raise ValueError(mode)


def _norm_matmul(x, g, w_bf16, mode):
    n = x.shape[0]
    tm = min(512, n)
    assert n % tm == 0
    ncol = w_bf16.shape[1]
    if mode == "glu":
        outs = [(D_MODEL, F32)]
    elif mode == "peer":
        outs = [(ncol, F32), (D_MODEL, BF16)]
    elif mode == "kv":
        outs = [(ncol, F32), (ncol, BF16)]
    elif mode == "nsa":
        outs = [(D_MODEL, F32), (GATE_COLS, F32)]
    res = pl.pallas_call(
        functools.partial(_norm_matmul_kernel, mode=mode),
        grid=(n // tm,),
        in_specs=[pl.BlockSpec((tm, D_MODEL), lambda i: (i, 0)),
                  pl.BlockSpec((1, D_MODEL), lambda i: (0, 0)),
                  pl.BlockSpec((D_MODEL, ncol), lambda i: (0, 0))],
        out_specs=[pl.BlockSpec((tm, c), lambda i: (i, 0)) for c, _ in outs],
        out_shape=[jax.ShapeDtypeStruct((n, c), dt) for c, dt in outs],
        compiler_params=_params("parallel"),
        name="norm_matmul_" + mode,
    )(x, g.reshape(1, D_MODEL), w_bf16)
    return res


def _matmul_res_kernel(a_ref, w_ref, h_ref, o_ref):
    o_ref[...] = h_ref[...] + jnp.dot(a_ref[...].astype(BF16), w_ref[...], preferred_element_type=F32)


def _matmul_res(a, w_bf16, h):
    n = h.shape[0]
    tm = min(512, n)
    return pl.pallas_call(
        _matmul_res_kernel,
        grid=(n // tm,),
        in_specs=[pl.BlockSpec((tm, D_MODEL), lambda i: (i, 0)),
                  pl.BlockSpec((D_MODEL, D_MODEL), lambda i: (0, 0)),
                  pl.BlockSpec((tm, D_MODEL), lambda i: (i, 0))],
        out_specs=pl.BlockSpec((tm, D_MODEL), lambda i: (i, 0)),
        out_shape=jax.ShapeDtypeStruct((n, D_MODEL), F32),
        compiler_params=_params("parallel"),
        name="attn_out_proj",
    )(a, w_bf16, h)


def _conv_kernel(prev_ref, u_ref, h_ref, dwk_ref, dwb_ref, lng_ref, lnb_ref, wout_ref, o_ref,
                 win_ref, zr_ref, z_ref, *, tt, chunk):
    win_ref[0:CONV_HALO, :] = prev_ref[0, 0]
    win_ref[CONV_HALO:CONV_HALO + tt, :] = u_ref[0]
    first_tap = CONV_HALO - (CONV_WIDTH - 1)

    win_ref[CONV_HALO + tt:, :] = jnp.zeros((CONV_PAD, D_MODEL), F32)
    n_z = tt + SUBLANES
    for r in range(SUBLANES):
        taps = [j for j in range(r, first_tap + CONV_WIDTH, SUBLANES) if j >= first_tap]
        for s0 in range(0, n_z, chunk):
            n = min(chunk, n_z - s0)
            z = None
            for j in taps:
                term = win_ref[s0 + j - r:s0 + j - r + n, :] * dwk_ref[j - first_tap:j - first_tap + 1, :]
                z = term if z is None else z + term
            zr_ref[r, s0:s0 + n, :] = z
    for r0 in range(0, tt, chunk):
        acc = zr_ref[0, r0:r0 + chunk, :]
        for r in range(1, SUBLANES):
            acc = acc + zr_ref[r, r0 + r:r0 + r + chunk, :]
        y = acc + dwb_ref[...]
        mu = jnp.mean(y, axis=-1, keepdims=True)
        yc = y - mu
        var = jnp.mean(yc * yc, axis=-1, keepdims=True)
        yn = yc * lax.rsqrt(var + EPS) * lng_ref[...] + lnb_ref[...]
        z_ref[r0:r0 + chunk, :] = yn * _sigmoid(yn)
    o_ref[0] = h_ref[0] + jnp.dot(z_ref[...].astype(BF16), wout_ref[...], preferred_element_type=F32)


def _conv_module(u, past, h, dw_k, dw_b, ln_g, ln_b, w_out_bf16):
    b, t, _ = u.shape
    tt = min(256, t)
    chunk = min(32, tt)
    nt = t // tt
    full = jnp.concatenate([jnp.zeros((b, CONV_HALO - (CONV_WIDTH - 1), D_MODEL), F32), past, u], axis=1)
    starts = np.arange(nt) * tt
    prev = jnp.stack([full[:, s:s + CONV_HALO] for s in starts], axis=1)
    vec = lambda a: a.reshape(1, D_MODEL)
    return pl.pallas_call(
        functools.partial(_conv_kernel, tt=tt, chunk=chunk),
        grid=(b, nt),
        in_specs=[pl.BlockSpec((1, 1, CONV_HALO, D_MODEL), lambda i, j: (i, j, 0, 0)),
                  pl.BlockSpec((1, tt, D_MODEL), lambda i, j: (i, j, 0)),
                  pl.BlockSpec((1, tt, D_MODEL), lambda i, j: (i, j, 0)),
                  pl.BlockSpec((CONV_WIDTH, D_MODEL), lambda i, j: (0, 0)),
                  pl.BlockSpec((1, D_MODEL), lambda i, j: (0, 0)),
                  pl.BlockSpec((1, D_MODEL), lambda i, j: (0, 0)),
                  pl.BlockSpec((1, D_MODEL), lambda i, j: (0, 0)),
                  pl.BlockSpec((D_MODEL, D_MODEL), lambda i, j: (0, 0))],
        out_specs=pl.BlockSpec((1, tt, D_MODEL), lambda i, j: (i, j, 0)),
        out_shape=jax.ShapeDtypeStruct((b, t, D_MODEL), F32),
        scratch_shapes=[pltpu.VMEM((CONV_HALO + tt + CONV_PAD, D_MODEL), F32),
                        pltpu.VMEM((SUBLANES, tt + SUBLANES, D_MODEL), F32),
                        pltpu.VMEM((tt, D_MODEL), F32)],
        compiler_params=_params("parallel", "parallel"),
        name="conformer_conv",
    )(prev, u, h, dw_k, vec(dw_b), vec(ln_g), vec(ln_b), w_out_bf16)


BF16_ROWS = 16


def _batcher_pairs(lo, hi):
    def merge(lo, hi, r):
        step = r * 2
        if step < hi - lo:
            yield from merge(lo, hi, step)
            yield from merge(lo + r, hi, step)
            yield from ((i, i + r) for i in range(lo + r, hi - r, step))
        else:
            yield (lo, lo + r)

    if hi - lo >= 1:
        mid = lo + (hi - lo) // 2
        yield from _batcher_pairs(lo, mid)
        yield from _batcher_pairs(mid + 1, hi)
        yield from merge(lo, hi, 1)


def _compare_exchange(x, i, j):
    x[i], x[j] = jnp.maximum(x[i], x[j]), jnp.minimum(x[i], x[j])


def _merge_sublanes(x):
    n = len(x)
    for shift in (4, 2, 1):
        y = [pltpu.roll(v, shift, 0) for v in x]
        x = [jnp.maximum(x[i], y[n - 1 - i]) for i in range(n)]
        d = n // 2
        while d:
            for i in range(n):
                if not i & d:
                    _compare_exchange(x, i, i + d)
            d //= 2
    return x


def _top16_sorted(s):
    x = [s[SUBLANES * j:SUBLANES * (j + 1)] for j in range(s.shape[0] // SUBLANES)]
    for i, j in _batcher_pairs(0, len(x) - 1):
        _compare_exchange(x, i, j)
    return _merge_sublanes(x)


def _bf16_bits(x):
    return pltpu.bitcast(x.astype(BF16).astype(F32), jnp.uint32)


def _bf16_pair(x):
    hi = _bf16_bits(x)
    return hi | (hi >> 16)


def _bf16_pack_rows(x):
    r, c = x.shape
    x4 = x.reshape(r // 16, 2, 8, c)
    words = _bf16_bits(x4[:, 1]) | (_bf16_bits(x4[:, 0]) >> 16)
    return words.reshape(r // 2, c)


def _bf16_unpack_rows(words):
    lo = pltpu.bitcast(words << 16, F32)
    hi = pltpu.bitcast(words & jnp.uint32(0xFFFF0000), F32)
    return jnp.stack([lo, hi], axis=1)


def _peer_topk_kernel(q_ref, k1_ref, k2_ref, n1_ref, e1_ref, r2_ref, e2_ref):
    half = PEER_DKEY // 2
    k = PEER_TOPK
    q = q_ref[...]
    s1 = lax.dot_general(k1_ref[...], q[:, :half], _NT, preferred_element_type=F32)
    s2 = lax.dot_general(k2_ref[...], q[:, half:], _NT, preferred_element_type=F32)
    v1 = _top16_sorted(s1)
    v2 = _top16_sorted(s2)
    sub = lax.broadcasted_iota(jnp.int32, v1[0].shape, 0)
    a_lo, a_hi = v1[0], v1[SUBLANES]
    n_b = jnp.full(sub.shape, k, jnp.int32)
    for r in range(1, SUBLANES):
        a_lo = jnp.where(sub == r, v1[r], a_lo)
        a_hi = jnp.where(sub == r, v1[SUBLANES + r], a_hi)
        n_b = jnp.where(sub == r, k // (r + 1), n_b)
    lists = [jnp.where(n_b > b, a_lo + v2[b], -jnp.inf) for b in range(k)]
    single = a_hi + v2[0]
    merged, x = [], single
    for b in range(k):
        merged.append(jnp.maximum(lists[b], x))
        x = jnp.minimum(lists[b], x)
    best = _merge_sublanes(merged)
    cmax, tau = best[0], best[-1]
    z = functools.reduce(jnp.add, [jnp.exp(t - cmax) for t in best])
    count_lo = functools.reduce(jnp.add, [jnp.where(t >= tau, 1.0, 0.0) for t in lists])
    count_hi = jnp.where(single >= tau, 1.0, 0.0)
    n1 = jnp.zeros(s1.shape, F32)
    for a in reversed(range(k)):
        r = a % SUBLANES
        n_a = (count_lo if a < SUBLANES else count_hi)[r:r + 1]
        n1 = jnp.where(s1 == v1[a][0:1], n_a, n1)
    rank2 = functools.reduce(jnp.add, [jnp.where(v2[b][0:1] > s2, 1.0, 0.0) for b in range(k)])
    n1_ref[0] = _bf16_pair(n1)
    e1_ref[0] = _bf16_pair(0.5 * jnp.exp(s1 - v1[0][0:1]) / z[0:1])
    r2_ref[0] = _bf16_pack_rows(rank2)
    e2_ref[0] = _bf16_pack_rows(jnp.exp(s2 - v2[0][0:1]))


def _peer_topk(q, sub_k):
    n = q.shape[0]
    tt = 256
    assert n % tt == 0
    row = pl.BlockSpec((1, PEER_KEYS, tt), lambda i, h: (h, 0, i))
    tile = pl.BlockSpec((1, PEER_KEYS // 2, tt), lambda i, h: (h, 0, i))
    row_shape = jax.ShapeDtypeStruct((PEER_HEADS, PEER_KEYS, n), jnp.uint32)
    tile_shape = jax.ShapeDtypeStruct((PEER_HEADS, PEER_KEYS // 2, n), jnp.uint32)
    return pl.pallas_call(
        _peer_topk_kernel,
        grid=(n // tt, PEER_HEADS),
        in_specs=[pl.BlockSpec((tt, PEER_DKEY), lambda i, h: (i, h)),
                  pl.BlockSpec((PEER_KEYS, PEER_DKEY // 2), lambda i, h: (0, 0)),
                  pl.BlockSpec((PEER_KEYS, PEER_DKEY // 2), lambda i, h: (0, 0))],
        out_specs=[row, row, tile, tile],
        out_shape=[row_shape, row_shape, tile_shape, tile_shape],
        compiler_params=_params("parallel", "parallel"),
        name="peer_topk",
    )(q, sub_k[0], sub_k[1])


PEER_HALF_I1 = 4
PEER_HALF = PEER_HALF_I1 * PEER_KEYS


def _peer_dense_kernel(xn_ref, u_ref, vta_ref, vtb_ref, n1_ref, e1_ref, r2_ref, e2_ref, h_ref, gout_ref,
                       o_ref, a_ref, wa_ref, wb_ref, acc_ref, *, tt, final_norm):
    e = pl.program_id(1)
    last = pl.num_programs(1) - 1

    @pl.when(e == 0)
    def _():
        acc_ref[...] = jnp.zeros_like(acc_ref)
        wb_ref[...] = jnp.zeros_like(wb_ref)

    live = jnp.where(e < last, jnp.uint32(1), jnp.uint32(0))
    n_tiles = PEER_KEYS // BF16_ROWS

    def packed_row(ref, hd, r, lanes, scale=None):
        word = ref[hd, r:r + 1, lanes]
        if scale is not None:
            word = word * scale
        return pltpu.bitcast(jnp.broadcast_to(word, (8, 128)), BF16)[None]

    def as_bf16_tiles(words):
        return pltpu.bitcast(words.reshape(n_tiles, 8, 128), BF16)

    def build(half, w_ref):
        rows_u = slice(half * PEER_HALF, (half + 1) * PEER_HALF)
        a_ref[...] = lax.dot_general(u_ref[rows_u, :], xn_ref[...], _NT, preferred_element_type=F32)
        pair = 2
        for lg in range(tt // 128):
            lanes = slice(lg * 128, (lg + 1) * 128)
            for il0 in range(0, PEER_HALF_I1, pair):
                gates = [jnp.zeros((n_tiles, BF16_ROWS, 128), BF16) for _ in range(pair)]
                for hd in range(PEER_HEADS):
                    r2 = as_bf16_tiles(r2_ref[hd, :, lanes])
                    e2 = as_bf16_tiles(e2_ref[hd, :, lanes])
                    for k in range(pair):
                        r = half * PEER_HALF_I1 + il0 + k
                        n1b = packed_row(n1_ref, hd, r, lanes, live)
                        e1b = packed_row(e1_ref, hd, r, lanes)
                        gates[k] = gates[k] + jnp.where(r2 < n1b, e2 * e1b, jnp.zeros_like(e2))
                for k in range(pair):
                    rows = slice((il0 + k) * PEER_KEYS, (il0 + k + 1) * PEER_KEYS)
                    gate = _bf16_unpack_rows(pltpu.bitcast(gates[k], jnp.uint32))
                    act = _gelu_tanh_doubled(a_ref[rows, lanes]).reshape(n_tiles, 2, 8, 128)
                    w_ref[rows, lanes] = (act * gate).reshape(PEER_KEYS, 128).astype(BF16)

    acc_ref[...] += jnp.dot(vtb_ref[...], wb_ref[...], preferred_element_type=F32)
    build(0, wa_ref)
    acc_ref[...] += jnp.dot(vta_ref[...], wa_ref[...], preferred_element_type=F32)
    build(1, wb_ref)

    @pl.when(e == last)
    def _():
        out = h_ref[...] + acc_ref[...].T
        if final_norm:
            out = _rms(out, gout_ref[...])
        o_ref[...] = out


def _peer_dense(xn_bf16, u_bf16, vt_bf16, n1, e1, r2, e2, h, g_out, final_norm):
    n = h.shape[0]
    tt = min(512, n)
    n_exp = u_bf16.shape[0]
    n_e = n_exp // (2 * PEER_HALF)
    tok = lambda i, e: (i, 0)
    cur = lambda e: jnp.minimum(e, n_e - 1)
    return pl.pallas_call(
        functools.partial(_peer_dense_kernel, tt=tt, final_norm=final_norm),
        grid=(n // tt, n_e + 1),
        in_specs=[pl.BlockSpec((tt, D_MODEL), tok),
                  pl.BlockSpec((2 * PEER_HALF, D_MODEL), lambda i, e: (cur(e), 0)),
                  pl.BlockSpec((D_MODEL, PEER_HALF), lambda i, e: (0, 2 * cur(e))),
                  pl.BlockSpec((D_MODEL, PEER_HALF), lambda i, e: (0, jnp.maximum(2 * e - 1, 0))),
                  pl.BlockSpec((PEER_HEADS, 2 * PEER_HALF_I1, tt), lambda i, e: (0, cur(e), i)),
                  pl.BlockSpec((PEER_HEADS, 2 * PEER_HALF_I1, tt), lambda i, e: (0, cur(e), i)),
                  pl.BlockSpec((PEER_HEADS, PEER_KEYS // 2, tt), lambda i, e: (0, 0, i)),
                  pl.BlockSpec((PEER_HEADS, PEER_KEYS // 2, tt), lambda i, e: (0, 0, i)),
                  pl.BlockSpec((tt, D_MODEL), tok),
                  pl.BlockSpec((1, D_MODEL), lambda i, e: (0, 0))],
        out_specs=pl.BlockSpec((tt, D_MODEL), tok),
        out_shape=jax.ShapeDtypeStruct((n, D_MODEL), F32),
        scratch_shapes=[pltpu.VMEM((PEER_HALF, tt), F32), pltpu.VMEM((PEER_HALF, tt), BF16),
                        pltpu.VMEM((PEER_HALF, tt), BF16), pltpu.VMEM((D_MODEL, tt), F32)],
        compiler_params=_params("parallel", "arbitrary"),
        name="peer_dense",
    )(xn_bf16, u_bf16, vt_bf16, vt_bf16, n1, e1, r2, e2, h, g_out.reshape(1, D_MODEL))


def _transpose_kernel(v_ref, o_ref):
    o_ref[...] = v_ref[...].T.astype(o_ref.dtype)


def _transpose_bf16(v):
    r, c = v.shape
    tr = 512
    return pl.pallas_call(
        _transpose_kernel,
        grid=(r // tr,),
        in_specs=[pl.BlockSpec((tr, c), lambda i: (i, 0))],
        out_specs=pl.BlockSpec((c, tr), lambda i: (0, i)),
        out_shape=jax.ShapeDtypeStruct((c, r), BF16),
        compiler_params=_params("parallel"),
        name="transpose_bf16",
    )(v)


def _peer_layer(h, g_norm, w_q, sub_k, u_emb, v_emb, g_out, final_norm):
    q, xn = _norm_matmul(h, g_norm, w_q.astype(BF16), "peer")
    n1, e1, r2, e2 = _peer_topk(q, sub_k)
    return _peer_dense(xn, u_emb.astype(BF16), _transpose_bf16(v_emb), n1, e1, r2, e2, h, g_out, final_norm)


GATHER_PAGES = 8
GATHER_ROWS = GATHER_PAGES * PAGE_SIZE


def _gather_kernel(pt_ref, *refs, n_steps):
    page_refs, new_ref, o_ref = refs[:GATHER_PAGES], refs[GATHER_PAGES], refs[GATHER_PAGES + 1]
    p = pl.program_id(1)

    @pl.when(p < n_steps)
    def _():
        for k, page_ref in enumerate(page_refs):
            o_ref[0, k * PAGE_SIZE:(k + 1) * PAGE_SIZE, :] = page_ref[0].T.astype(o_ref.dtype)

    @pl.when(p >= n_steps)
    def _():
        o_ref[0] = new_ref[0].astype(o_ref.dtype)


def _gather_pages(cache, page_table, new_rows, out_dtype):
    b, n_pages = page_table.shape
    assert n_pages % GATHER_PAGES == 0 and new_rows.shape[1] <= GATHER_ROWS
    n_steps = n_pages // GATHER_PAGES
    new_pad = jnp.pad(new_rows, ((0, 0), (0, GATHER_ROWS - new_rows.shape[1]), (0, 0)))

    def page_spec(k):
        return pl.BlockSpec((1, KV_COLS, PAGE_SIZE),
                            lambda i, p, pt: (pt[i, jnp.minimum(p, n_steps - 1) * GATHER_PAGES + k], 0, 0))

    grid_spec = pltpu.PrefetchScalarGridSpec(
        num_scalar_prefetch=1,
        grid=(b, n_steps + 1),
        in_specs=[page_spec(k) for k in range(GATHER_PAGES)]
        + [pl.BlockSpec((1, GATHER_ROWS, KV_COLS), lambda i, p, pt: (i, 0, 0))],
        out_specs=pl.BlockSpec((1, GATHER_ROWS, KV_COLS), lambda i, p, pt: (i, p, 0)),
    )
    return pl.pallas_call(
        functools.partial(_gather_kernel, n_steps=n_steps),
        grid_spec=grid_spec,
        out_shape=jax.ShapeDtypeStruct((b, (n_steps + 1) * GATHER_ROWS, KV_COLS), out_dtype),
        compiler_params=_params("parallel", "arbitrary"),
        name="gather_pages",
    )(page_table, *([cache] * GATHER_PAGES), new_pad)


CMP_TILE = 128


CMP_TOKENS = CMP_TILE * CMP_STRIDE
CMP_LANE_TILES = KV_COLS // 128
CMP_PAGES = CMP_TOKENS // PAGE_SIZE


def _compress_kernel(main_ref, halo_ref, pe_ref, w1k_ref, w1v_ref, w2k_ref, w2v_ref, o_ref, win_ref):
    for c in range(CMP_LANE_TILES):
        win_ref[c, 0:CMP_TOKENS, :] = main_ref[0, :, c * 128:(c + 1) * 128]
        win_ref[c, CMP_TOKENS:CMP_TOKENS + CMP_STRIDE, :] = halo_ref[0, :, c * 128:(c + 1) * 128]
    _compress_window(win_ref, pe_ref, w1k_ref, w1v_ref, w2k_ref, w2v_ref, o_ref)


def _compress_paged_kernel(pt_ref, *refs):
    page_refs, halo_ref = refs[:CMP_PAGES], refs[CMP_PAGES]
    pe_ref, w1k_ref, w1v_ref, w2k_ref, w2v_ref, o_ref, win_ref = refs[CMP_PAGES + 1:]
    for c in range(CMP_LANE_TILES):
        cols = slice(c * 128, (c + 1) * 128)
        for k, page_ref in enumerate(page_refs):
            win_ref[c, k * PAGE_SIZE:(k + 1) * PAGE_SIZE, :] = page_ref[0, cols, :].T
        win_ref[c, CMP_TOKENS:CMP_TOKENS + CMP_STRIDE, :] = halo_ref[0, cols, :].T[0:CMP_STRIDE]
    _compress_window(win_ref, pe_ref, w1k_ref, w1v_ref, w2k_ref, w2v_ref, o_ref)


def _compress_window(win_ref, pe_ref, w1k_ref, w1v_ref, w2k_ref, w2v_ref, o_ref):
    n_lane_tiles = CMP_LANE_TILES
    hk = jnp.zeros((CMP_TILE, N_KV_HEADS * CMP_HIDDEN), F32)
    hv = jnp.zeros((CMP_TILE, N_KV_HEADS * CMP_HIDDEN), F32)
    for s in range(CMP_BLOCK):
        x = jnp.concatenate([win_ref[c, pl.ds(s, CMP_TILE, stride=CMP_STRIDE), :] for c in range(n_lane_tiles)],
                            axis=1)
        x = (x + pe_ref[s]).astype(BF16)
        hk = hk + jnp.dot(x[:, :K_COLS], w1k_ref[s], preferred_element_type=F32)
        hv = hv + jnp.dot(x[:, K_COLS:], w1v_ref[s], preferred_element_type=F32)
    o_ref[0, :, :K_COLS] = jnp.dot(_gelu_tanh(hk).astype(BF16), w2k_ref[...], preferred_element_type=F32)
    o_ref[0, :, K_COLS:] = jnp.dot(_gelu_tanh(hv).astype(BF16), w2v_ref[...], preferred_element_type=F32)


def _block_diag(w):
    eye = jnp.eye(N_KV_HEADS, dtype=w.dtype)
    out = jnp.einsum("gh,...ab->...gahb", eye, w)
    return out.reshape(w.shape[:-2] + (N_KV_HEADS * w.shape[-2], N_KV_HEADS * w.shape[-1]))


def _compress(rows, n_tiles, cmp_pe, cmp_w1, cmp_w2):
    b, length = rows.shape[:2]
    last_halo = length // CMP_STRIDE - 1
    weights, weight_specs = _compress_weights(cmp_pe, cmp_w1, cmp_w2, lambda shape: (lambda i, j: (0,) * len(shape)))
    return pl.pallas_call(
        _compress_kernel,
        grid=(b, n_tiles),
        in_specs=[pl.BlockSpec((1, CMP_TOKENS, KV_COLS), lambda i, j: (i, j, 0)),
                  pl.BlockSpec((1, CMP_STRIDE, KV_COLS),
                               lambda i, j: (i, jnp.minimum((j + 1) * CMP_TILE, last_halo), 0))] + weight_specs,
        out_specs=pl.BlockSpec((1, CMP_TILE, KV_COLS), lambda i, j: (i, j, 0)),
        out_shape=jax.ShapeDtypeStruct((b, n_tiles * CMP_TILE, KV_COLS), F32),
        scratch_shapes=[pltpu.VMEM((CMP_LANE_TILES, CMP_TOKENS + CMP_STRIDE, 128), F32)],
        compiler_params=_params("parallel", "parallel"),
        name="cmp_compress",
    )(rows, rows, *weights)


def _compress_weights(cmp_pe, cmp_w1, cmp_w2, const_map):
    w1 = cmp_w1.reshape(2, CMP_BLOCK, HEAD_DIM, CMP_HIDDEN)
    w1k, w1v = _block_diag(w1[0]).astype(BF16), _block_diag(w1[1]).astype(BF16)
    w2k, w2v = _block_diag(cmp_w2[0]).astype(BF16), _block_diag(cmp_w2[1]).astype(BF16)
    pe = jnp.concatenate([jnp.tile(cmp_pe[0], (1, N_KV_HEADS)), jnp.tile(cmp_pe[1], (1, N_KV_HEADS))], axis=1)
    weights = [pe.reshape(CMP_BLOCK, 1, KV_COLS), w1k, w1v, w2k, w2v]
    return weights, [pl.BlockSpec(w.shape, const_map(w.shape)) for w in weights]


def _compress_paged(cache, page_table, n_tiles, cmp_pe, cmp_w1, cmp_w2):
    b, n_pages = page_table.shape
    assert n_tiles * CMP_PAGES <= n_pages
    weights, weight_specs = _compress_weights(cmp_pe, cmp_w1, cmp_w2,
                                              lambda shape: (lambda i, j, pt: (0,) * len(shape)))

    def page_spec(k):
        return pl.BlockSpec((1, KV_COLS, PAGE_SIZE),
                            lambda i, j, pt: (pt[i, jnp.minimum(j * CMP_PAGES + k, n_pages - 1)], 0, 0))

    grid_spec = pltpu.PrefetchScalarGridSpec(
        num_scalar_prefetch=1,
        grid=(b, n_tiles),
        in_specs=[page_spec(k) for k in range(CMP_PAGES + 1)] + weight_specs,
        out_specs=pl.BlockSpec((1, CMP_TILE, KV_COLS), lambda i, j, pt: (i, j, 0)),
        scratch_shapes=[pltpu.VMEM((CMP_LANE_TILES, CMP_TOKENS + CMP_STRIDE, 128), F32)],
    )
    return pl.pallas_call(
        _compress_paged_kernel,
        grid_spec=grid_spec,
        out_shape=jax.ShapeDtypeStruct((b, n_tiles * CMP_TILE, KV_COLS), F32),
        compiler_params=_params("parallel", "parallel"),
        name="cmp_compress_paged",
    )(page_table, *([cache] * (CMP_PAGES + 1)), *weights)


def _head_perm():
    c = np.arange(D_MODEL)
    h, g, d = c // K_COLS, (c % K_COLS) // HEAD_DIM, c % HEAD_DIM
    return (g * HEADS_PER_KV + h) * HEAD_DIM + d


def _gate_expand(branch):
    c = np.arange(D_MODEL)
    h, g = c // K_COLS, (c % K_COLS) // HEAD_DIM
    m = np.zeros((GATE_COLS, D_MODEL), np.float32)
    m[(g * HEADS_PER_KV + h) * N_BRANCH + branch, c] = 1.0
    return jnp.asarray(m)


def _block_diag_queries(q):
    tq = q.shape[0]
    lane_g = lax.broadcasted_iota(jnp.int32, (tq, K_COLS), 1) // HEAD_DIM
    blocks = []
    for g in range(N_KV_HEADS):
        for h in range(HEADS_PER_KV):
            blocks.append(jnp.where(lane_g == g, q[:, h * K_COLS:(h + 1) * K_COLS], 0.0))
    return jnp.concatenate(blocks, axis=0)


def _diag_heads(acc, tq):
    lane_g = lax.broadcasted_iota(jnp.int32, (tq, K_COLS), 1) // HEAD_DIM
    outs = []
    for h in range(HEADS_PER_KV):
        o = jnp.zeros((tq, K_COLS), F32)
        for g in range(N_KV_HEADS):
            r0 = (g * HEADS_PER_KV + h) * tq
            o = o + jnp.where(lane_g == g, acc[r0:r0 + tq], 0.0)
        outs.append(o)
    return jnp.concatenate(outs, axis=1)


_NT = (((1,), (1,)), ((), ()))


def _cmp_attn_kernel(q_ref, gate_ref, kvc_ref, ovt_ref, eg_ref, oc_ref, sel_ref, sc_ref,
                     *, tq, pos0, n_slc_loop):
    q0 = pos0 + pl.program_id(1) * tq
    n_pad = kvc_ref.shape[1]
    s_pad = ovt_ref.shape[0]
    rows = N_HEADS * tq
    qbd = _block_diag_queries(q_ref[0]).astype(BF16)
    kc = kvc_ref[0, :, :K_COLS].astype(BF16)
    vc = kvc_ref[0, :, K_COLS:].astype(BF16)
    s = lax.dot_general(qbd, kc, _NT, preferred_element_type=F32)
    t_row = q0 + (lax.broadcasted_iota(jnp.int32, (rows, n_pad), 0) & (tq - 1))
    n_col = lax.broadcasted_iota(jnp.int32, (rows, n_pad), 1)
    vis = (n_col * CMP_STRIDE + (CMP_BLOCK - 1)) <= t_row
    m = jnp.max(jnp.where(vis, s, -jnp.inf), axis=-1, keepdims=True)
    m = jnp.where(m > -jnp.inf, m, 0.0)
    e = jnp.where(vis, jnp.exp(s - m), 0.0)
    p = e / jnp.maximum(jnp.sum(e, axis=-1, keepdims=True), 1e-30)
    oc = _diag_heads(jnp.dot(p.astype(BF16), vc, preferred_element_type=F32), tq)
    oc_ref[0] = oc * jnp.dot(gate_ref[0], eg_ref[...], preferred_element_type=F32)

    psum = jnp.concatenate(
        [sum(p[(g * HEADS_PER_KV + h) * tq:(g * HEADS_PER_KV + h + 1) * tq] for h in range(HEADS_PER_KV))
         for g in range(N_KV_HEADS)], axis=0)
    imp_t = lax.dot_general(ovt_ref[...], psum, _NT, preferred_element_type=F32)
    cols = N_KV_HEADS * tq
    blk = lax.broadcasted_iota(jnp.int32, (s_pad, cols), 0)
    cur = (q0 + (lax.broadcasted_iota(jnp.int32, (s_pad, cols), 1) & (tq - 1))) // SLC_BLOCK
    forced = (blk == 0) | (blk == cur) | (blk == cur - 1)
    score = jnp.where(forced, jnp.inf, jnp.where(blk <= cur, imp_t, -jnp.inf))
    sc_ref[...] = score

    score = score[:n_slc_loop]
    blk = lax.broadcasted_iota(jnp.int32, (n_slc_loop, cols), 0)
    cur = (q0 + (lax.broadcasted_iota(jnp.int32, (n_slc_loop, cols), 1) & (tq - 1))) // SLC_BLOCK

    def rank_body(jb, rank):
        rows8 = sc_ref[pl.ds(pl.multiple_of(jb * 8, 8), 8), :]
        for r in range(8):
            row = rows8[r:r + 1]
            tie = jnp.where(row == score, jnp.where(jb * 8 + r < blk, 1.0, 0.0), 0.0)
            rank = rank + jnp.where(row > score, 1.0, tie)
        return rank

    rank = lax.fori_loop(0, n_slc_loop // 8, rank_body, jnp.zeros((n_slc_loop, cols), F32))
    sel_t = jnp.where(rank < float(N_SELECT), jnp.where(blk <= cur, 1.0, 0.0), 0.0)
    if n_slc_loop < s_pad:
        sel_t = jnp.concatenate([sel_t, jnp.zeros((s_pad - n_slc_loop, cols), F32)], axis=0)
    sel_t = sel_t.astype(BF16)
    ci = lax.broadcasted_iota(jnp.int32, (cols, cols), 0)
    cj = lax.broadcasted_iota(jnp.int32, (cols, cols), 1)
    eye = jnp.where(ci == cj, 1.0, 0.0).astype(BF16)
    sel_ref[0, 0] = lax.dot_general(eye, sel_t, _NT, preferred_element_type=F32)


def _cmp_attn(q, gates, kvc, ov_t, pos0, tq, n_slc):
    b, t, _ = q.shape
    s_pad = ov_t.shape[0]
    n_pad = kvc.shape[1]
    n_slc_loop = -(-n_slc // 8) * 8
    return pl.pallas_call(
        functools.partial(_cmp_attn_kernel, tq=tq, pos0=pos0, n_slc_loop=n_slc_loop),
        grid=(b, t // tq),
        in_specs=[pl.BlockSpec((1, tq, D_MODEL), lambda i, j: (i, j, 0)),
                  pl.BlockSpec((1, tq, GATE_COLS), lambda i, j: (i, j, 0)),
                  pl.BlockSpec((1, n_pad, KV_COLS), lambda i, j: (i, 0, 0)),
                  pl.BlockSpec((s_pad, n_pad), lambda i, j: (0, 0)),
                  pl.BlockSpec((GATE_COLS, D_MODEL), lambda i, j: (0, 0))],
        out_specs=[pl.BlockSpec((1, tq, D_MODEL), lambda i, j: (i, j, 0)),
                   pl.BlockSpec((1, 1, N_KV_HEADS * tq, s_pad), lambda i, j: (i, j, 0, 0))],
        out_shape=[jax.ShapeDtypeStruct((b, t, D_MODEL), F32),
                   jax.ShapeDtypeStruct((b, t // tq, N_KV_HEADS * tq, s_pad), F32)],
        scratch_shapes=[pltpu.VMEM((s_pad, N_KV_HEADS * tq), F32)],
        compiler_params=_params("parallel", "parallel"),
        name="nsa_cmp_select",
    )(q, gates, kvc, ov_t, _gate_expand(0))


WIN_KEYS = 768
ATTN_ROWS = 64


def _lane_tiles(x):
    return [x[:, i * 128:(i + 1) * 128] for i in range(x.shape[1] // 128)]


def _row_max(x):
    return jnp.max(functools.reduce(jnp.maximum, _lane_tiles(x)), axis=-1, keepdims=True)


def _row_sum(x):
    return jnp.sum(functools.reduce(jnp.add, _lane_tiles(x)), axis=-1, keepdims=True)


def _sparse_attn_kernel(q_ref, gate_ref, sel_ref, oc_ref, kslc_ref, kwin_ref, egs_ref, egw_ref, o_ref,
                        qaug_ref, s_ref, p_ref, sw_ref, pw_ref, m_ref, l_ref, alpha_ref, acc_ref,
                        *, tq, tk, pos0, win_pos0):
    q0 = pos0 + pl.program_id(1) * tq
    rows = N_HEADS * tq
    s_pad = sel_ref.shape[3]
    rb = min(ATTN_ROWS, rows)

    qaug_ref[:, 0:K_COLS] = _block_diag_queries(q_ref[0]).astype(BF16)
    not_sel = sel_ref[0, 0] - 1.0
    qaug_ref[:, K_COLS:] = jnp.concatenate(
        [not_sel[g * tq:(g + 1) * tq] for g in range(N_KV_HEADS) for _ in range(HEADS_PER_KV)],
        axis=0).astype(BF16)

    def q_pos(r0, width):
        return q0 + ((r0 + lax.broadcasted_iota(jnp.int32, (rb, width), 0)) & (tq - 1))

    m_ref[...] = jnp.full_like(m_ref, NEG_BIG)
    l_ref[...] = jnp.zeros_like(l_ref)
    acc_ref[...] = jnp.zeros_like(acc_ref)
    blk_lane = lax.broadcasted_iota(jnp.int32, (tk, s_pad), 1)
    blk_key = lax.broadcasted_iota(jnp.int32, (tk, s_pad), 0) // SLC_BLOCK
    k_col = lax.broadcasted_iota(jnp.int32, (rb, tk), 1)

    n_split = s_ref.shape[0]
    part = rows // n_split

    def slc_tile(j, causal):
        ks = pl.ds(pl.multiple_of(j * tk, tk), tk)
        onehot = jnp.where(blk_lane == j * (tk // SLC_BLOCK) + blk_key, -NEG_BIG, 0.0).astype(BF16)
        kaug = jnp.concatenate([kslc_ref[0, ks, 0:K_COLS], onehot], axis=1)
        vt = kslc_ref[0, ks, K_COLS:KV_COLS]
        for sp in range(n_split):
            s_ref[sp] = lax.dot_general(qaug_ref[sp * part:(sp + 1) * part, :], kaug, _NT,
                                        preferred_element_type=F32)
        for sp in range(n_split):
            for r0 in range(0, part, rb):
                blk = slice(sp * part + r0, sp * part + r0 + rb)
                s = s_ref[sp, r0:r0 + rb, :]
                if causal:
                    s = jnp.where(j * tk + k_col <= q_pos(sp * part + r0, tk), s, NEG_BIG)
                m_old = m_ref[blk, :]
                m_new = jnp.maximum(m_old, _row_max(s))
                p = jnp.exp(s - m_new)
                alpha = jnp.exp(m_old - m_new)
                l_ref[blk, :] = alpha * l_ref[blk, :] + _row_sum(p)
                m_ref[blk, :] = m_new
                alpha_ref[blk, :] = alpha
                p_ref[sp, r0:r0 + rb, :] = p.astype(BF16)
            prows = slice(sp * part, (sp + 1) * part)
            acc_ref[prows, :] = alpha_ref[prows, :] * acc_ref[prows, :] + jnp.dot(
                p_ref[sp], vt, preferred_element_type=F32)

    last_tile = (q0 + tq - 1) // tk
    lax.fori_loop(0, last_tile, lambda j, c: (slc_tile(j, False), c)[1], 0)
    slc_tile(last_tile, True)
    o_s = _diag_heads(acc_ref[...] / jnp.maximum(l_ref[...], 1e-30), tq)

    lw = kwin_ref.shape[1]
    w0 = jnp.clip((q0 - win_pos0 - WINDOW) // 128 * 128, 0, lw - WIN_KEYS)
    wk = pl.ds(pl.multiple_of(w0, 128), WIN_KEYS)
    kw = kwin_ref[0, wk, 0:K_COLS]
    vw = kwin_ref[0, wk, K_COLS:KV_COLS]
    for sp in range(n_split):
        sw_ref[sp] = lax.dot_general(qaug_ref[sp * part:(sp + 1) * part, 0:K_COLS], kw, _NT,
                                     preferred_element_type=F32)
    kpos = win_pos0 + w0 + lax.broadcasted_iota(jnp.int32, (rb, WIN_KEYS), 1)
    for sp in range(n_split):
        for r0 in range(0, part, rb):
            blk = slice(sp * part + r0, sp * part + r0 + rb)
            t = q_pos(sp * part + r0, WIN_KEYS)
            s = jnp.where((kpos <= t) & (kpos > t - WINDOW), sw_ref[sp, r0:r0 + rb, :], NEG_BIG)
            p = jnp.exp(s - _row_max(s))
            l_ref[blk, :] = _row_sum(p)
            pw_ref[sp, r0:r0 + rb, :] = p.astype(BF16)
        prows = slice(sp * part, (sp + 1) * part)
        acc_ref[prows, :] = jnp.dot(pw_ref[sp], vw, preferred_element_type=F32)
    o_w = _diag_heads(acc_ref[...] / jnp.maximum(l_ref[...], 1e-30), tq)

    gate = gate_ref[0]
    g_s = jnp.dot(gate, egs_ref[...], preferred_element_type=F32)
    g_w = jnp.dot(gate, egw_ref[...], preferred_element_type=F32)
    o_ref[0] = (oc_ref[0] + g_s * o_s + g_w * o_w).astype(o_ref.dtype)


def _sparse_attn(q, gates, sel, oc, kslc, kwin, slc_col, win_col, pos0, win_pos0, tq, tk):
    b, t, _ = q.shape
    s_pad = sel.shape[3]
    ls, lw = kslc.shape[1], kwin.shape[1]
    assert ls % tk == 0 and lw % 128 == 0 and lw >= WIN_KEYS and tk % tq == 0
    rows = N_HEADS * tq
    n_split = max(1, min(4, rows // (2 * ATTN_ROWS)))
    qtile = lambda i, j: (i, j, 0)
    return pl.pallas_call(
        functools.partial(_sparse_attn_kernel, tq=tq, tk=tk, pos0=pos0, win_pos0=win_pos0),
        grid=(b, t // tq),
        in_specs=[pl.BlockSpec((1, tq, D_MODEL), qtile),
                  pl.BlockSpec((1, tq, GATE_COLS), qtile),
                  pl.BlockSpec((1, 1, N_KV_HEADS * tq, s_pad), lambda i, j: (i, j, 0, 0)),
                  pl.BlockSpec((1, tq, D_MODEL), qtile),
                  pl.BlockSpec((1, ls, KV_COLS), lambda i, j: (i, 0, slc_col)),
                  pl.BlockSpec((1, lw, KV_COLS), lambda i, j: (i, 0, win_col)),
                  pl.BlockSpec((GATE_COLS, D_MODEL), lambda i, j: (0, 0)),
                  pl.BlockSpec((GATE_COLS, D_MODEL), lambda i, j: (0, 0))],
        out_specs=pl.BlockSpec((1, tq, D_MODEL), qtile),
        out_shape=jax.ShapeDtypeStruct((b, t, D_MODEL), F32),
        scratch_shapes=[pltpu.VMEM((rows, K_COLS + s_pad), BF16),
                        pltpu.VMEM((n_split, rows // n_split, tk), F32),
                        pltpu.VMEM((n_split, rows // n_split, tk), BF16),
                        pltpu.VMEM((n_split, rows // n_split, WIN_KEYS), F32),
                        pltpu.VMEM((n_split, rows // n_split, WIN_KEYS), BF16),
                        pltpu.VMEM((rows, 1), F32), pltpu.VMEM((rows, 1), F32), pltpu.VMEM((rows, 1), F32),
                        pltpu.VMEM((rows, K_COLS), F32)],
        compiler_params=_params("parallel", "arbitrary"),
        name="nsa_select_window",
    )(q, gates, sel, oc, kslc, kwin, _gate_expand(1), _gate_expand(2))


ATTN_TK = 1024


def _overlap_t(s_pad, n_pad, n_slc, n_cmp):
    n = np.arange(n_pad)[None, :]
    s = np.arange(s_pad)[:, None]
    ov = ((n * CMP_STRIDE < s * SLC_BLOCK + SLC_BLOCK) & (n * CMP_STRIDE + CMP_BLOCK - 1 >= s * SLC_BLOCK)
          & (n < n_cmp) & (s < n_slc))
    return jnp.asarray(ov.astype(np.float32))


def _trunk(x, conv_past, past, pos0, p):
    b, t, _ = x.shape
    n = b * t
    kvh = (2, N_KV_HEADS, HEAD_DIM)
    h = x.reshape(n, D_MODEL)

    (u,) = _norm_matmul(h, p["norm_mix_g"][0], p["conv_w_in"][0].astype(BF16), "glu")
    u = u.reshape(b, t, D_MODEL)
    h = _conv_module(u, conv_past, h.reshape(b, t, D_MODEL), p["conv_dw_k"][0], p["conv_dw_b"][0],
                     p["conv_ln_g"][0], p["conv_ln_b"][0], p["conv_w_out"][0].astype(BF16)).reshape(n, D_MODEL)
    conv_state = jnp.concatenate([conv_past, u], axis=1)[:, -(CONV_WIDTH - 1):][None]
    h = _peer_layer(h, p["norm_ffn_g"][0], p["peer_w_q"][0], p["peer_sub_k"][0], p["peer_u"][0],
                    p["peer_v"][0], p["norm_out_g"], False)

    kv, kv_bf16 = _norm_matmul(h, p["norm_kv_g"], p["nsa_w_kv"].astype(BF16), "kv")
    kv = kv.reshape(b, t, N_BRANCH * KV_COLS)
    kv_bf16 = kv_bf16.reshape(b, t, N_BRANCH * KV_COLS)
    new_cmp, new_slc, new_win = (kv[..., i * KV_COLS:(i + 1) * KV_COLS] for i in range(N_BRANCH))
    if past is None:
        length, win_pos0 = t, pos0
        cmp_rows, k_slc, k_win, slc_col, win_col = kv, kv_bf16, kv_bf16, 1, 2
        win_all = new_win
    else:
        cache_cmp, cache_slc, cache_win, page_table = past
        length, win_pos0 = pos0 + t, pos0 - cache_win.shape[1]
        k_slc = _gather_pages(cache_slc, page_table, new_slc, BF16)
        win_all = jnp.concatenate([cache_win, new_win], axis=1)
        lw = max(WIN_KEYS, -(-win_all.shape[1] // 128) * 128)
        k_win = jnp.pad(win_all, ((0, 0), (0, lw - win_all.shape[1]), (0, 0))).astype(BF16)
        slc_col, win_col = 0, 0
    n_cmp = (length - CMP_BLOCK) // CMP_STRIDE + 1
    n_slc = -(-length // SLC_BLOCK)
    n_tiles = -(-n_cmp // CMP_TILE)
    if past is None:
        kvc = _compress(cmp_rows, n_tiles, p["nsa_cmp_pe"], p["nsa_cmp_w1"], p["nsa_cmp_w2"])
    else:
        assert (n_cmp - 1) * CMP_STRIDE + CMP_BLOCK <= page_table.shape[1] * PAGE_SIZE
        kvc = _compress_paged(cache_cmp, page_table, n_tiles, p["nsa_cmp_pe"], p["nsa_cmp_w1"], p["nsa_cmp_w2"])

    perm = _head_perm()
    w_in = p["nsa_w_in"][0]
    w_in = jnp.concatenate([w_in[:, perm], w_in[:, D_MODEL:],
                            jnp.zeros((D_MODEL, GATE_COLS - N_HEADS * N_BRANCH), F32)], axis=1).astype(BF16)
    q, gates = _norm_matmul(h, p["norm_mix_g"][1], w_in, "nsa")
    q = q.reshape(b, t, D_MODEL)
    gates = gates.reshape(b, t, GATE_COLS)
    tq = min(64, t)
    s_pad = -(-n_slc // 128) * 128
    oc, sel = _cmp_attn(q, gates, kvc, _overlap_t(s_pad, kvc.shape[1], n_slc, n_cmp), pos0, tq, n_slc)
    o = _sparse_attn(q, gates, sel, oc, k_slc, k_win, slc_col, win_col, pos0, win_pos0, tq, ATTN_TK)
    h = _matmul_res(o.reshape(n, D_MODEL), p["nsa_w_o"][0][perm, :].astype(BF16), h)
    y = _peer_layer(h, p["norm_ffn_g"][1], p["peer_w_q"][1], p["peer_sub_k"][1], p["peer_u"][1],
                    p["peer_v"][1], p["norm_out_g"], True)
    keep = min(WINDOW, win_all.shape[1])
    shape5 = lambda a: a.reshape(a.shape[:2] + kvh)
    return (y.reshape(b, t, D_MODEL), conv_state, shape5(new_cmp), shape5(new_slc), shape5(win_all[:, -keep:]))


def kernel(x_prompt, x_sample, state_conv, cache_cmp_kv, cache_slc_kv, cache_win_kv, page_table, norm_mix_g, norm_ffn_g, norm_kv_g, norm_out_g, conv_w_in, conv_dw_k, conv_dw_b, conv_ln_g, conv_ln_b, conv_w_out, nsa_w_kv, nsa_cmp_pe, nsa_cmp_w1, nsa_cmp_w2, nsa_w_in, nsa_w_o, peer_w_q, peer_sub_k, peer_u, peer_v):
    p = dict(norm_mix_g=norm_mix_g, norm_ffn_g=norm_ffn_g, norm_kv_g=norm_kv_g, norm_out_g=norm_out_g,
             conv_w_in=conv_w_in, conv_dw_k=conv_dw_k, conv_dw_b=conv_dw_b, conv_ln_g=conv_ln_g,
             conv_ln_b=conv_ln_b, conv_w_out=conv_w_out, nsa_w_kv=nsa_w_kv, nsa_cmp_pe=nsa_cmp_pe,
             nsa_cmp_w1=nsa_cmp_w1, nsa_cmp_w2=nsa_cmp_w2, nsa_w_in=nsa_w_in, nsa_w_o=nsa_w_o,
             peer_w_q=peer_w_q, peer_sub_k=peer_sub_k, peer_u=peer_u, peer_v=peer_v)
    bsz = x_prompt.shape[0]
    conv0 = jnp.zeros((bsz, CONV_WIDTH - 1, D_MODEL), x_prompt.dtype)
    y_p, conv_p, cmp_p, slc_p, win_p = _trunk(x_prompt, conv0, None, 0, p)
    dec_b, n_pages = page_table.shape
    flat = lambda c: c.reshape(c.shape[0], c.shape[1], KV_COLS)
    paged = lambda c: c.transpose(0, 2, 3, 4, 1).reshape(c.shape[0], KV_COLS, c.shape[1])
    past = (paged(cache_cmp_kv), paged(cache_slc_kv), flat(cache_win_kv), page_table)
    y_s, conv_s, cmp_s, slc_s, win_s = _trunk(x_sample, state_conv[0], past, n_pages * PAGE_SIZE, p)
    return (y_p, y_s, conv_p, cmp_p, slc_p, win_p, conv_s, cmp_s, slc_s, win_s)
```
